```python
import math
import jax, jax.numpy as jnp
from jax import lax
import numpy as np

D_MODEL = 1024
BATCH = 8
SEQ = 2048
DEPTH = 4
DEC_BATCH = 128
DEC_SEQ = 4
PAST_LEN = 8192
PAGE_SIZE = 128

N_MIXERS = 2
N_CHUNK_LAYERS = (DEPTH + 1) // 2
N_SWA_LAYERS = DEPTH // 2
CHUNK = 128
D_HALF_A = 2 * D_MODEL
N_GROUPS_A = 8
D_GROUP_A = D_HALF_A // N_GROUPS_A
N_HEADS = 16
N_KV_HEADS = 4
HEAD_DIM = 64
Q_PER_KV = N_HEADS // N_KV_HEADS
WINDOW = 128
NUM_BUCKETS = 32
MAX_DISTANCE = 128
N_EXPERTS = 32
TOP_K = 4
D_FF_E = D_MODEL
SWIGLU_LIMIT = 7.0
SWIGLU_ALPHA = 1.702
EPS = 1e-6
NEG_INF = -1e30

kernel_name = "hybrid_sgu_swa_moe_decode_step"


def rms_norm(x, g):
    xf = x.astype(jnp.float32)
    xf = xf * lax.rsqrt(jnp.mean(jnp.square(xf), axis=-1, keepdims=True) + EPS)
    return (xf * g.astype(jnp.float32)).astype(x.dtype)


def layer_norm(x, g, b):
    xf = x.astype(jnp.float32)
    mu = jnp.mean(xf, axis=-1, keepdims=True)
    xc = xf - mu
    xf = xc * lax.rsqrt(jnp.mean(jnp.square(xc), axis=-1, keepdims=True) + EPS)
    return (xf * g.astype(jnp.float32) + b.astype(jnp.float32)).astype(x.dtype)


def chunk_sgu(xn, w_in, b_in, ln_g, ln_b, w_s, b_s, w_out, b_out):
    bsz, s_len = xn.shape[0], xn.shape[1]
    c = min(s_len, CHUNK)
    nc = s_len // c
    z = jax.nn.gelu(xn @ w_in + b_in, approximate=False)
    u, v = z[..., :D_HALF_A], z[..., D_HALF_A:]
    v = layer_norm(v, ln_g, ln_b)
    w = jnp.tril(w_s[:, :c, :c]).astype(v.dtype)
    vg = v.reshape(bsz, nc, c, N_GROUPS_A, D_GROUP_A)
    gate = jnp.einsum('gpq,bnqgd->bnpgd', w, vg) + b_s[:, :c].T[:, :, None]
    out = (u.reshape(bsz, nc, c, N_GROUPS_A, D_GROUP_A) * gate).reshape(bsz, s_len, D_HALF_A)
    return out @ w_out + b_out, v


def t5_bucket(dist):
    n = jnp.maximum(dist, 0)
    max_exact = NUM_BUCKETS // 2
    nf = jnp.maximum(n, max_exact).astype(jnp.float32)
    large = max_exact + (jnp.log(nf / max_exact) / math.log(MAX_DISTANCE / max_exact)
                         * (NUM_BUCKETS - max_exact)).astype(jnp.int32)
    large = jnp.minimum(large, NUM_BUCKETS - 1)
    return jnp.where(n < max_exact, n, large)


def window_bias(dist, rel_bias):
    tq, tk = dist.shape
    bias = rel_bias[t5_bucket(dist)].astype(jnp.float32)
    bias = jnp.transpose(bias, (2, 0, 1)).reshape(N_KV_HEADS, Q_PER_KV, tq, tk)
    valid = (dist >= 0) & (dist < WINDOW)
    return jnp.where(valid, bias, NEG_INF)


def swa_qkv(xn, w_qkv, b_qkv, q_g, k_g):
    lead = xn.shape[:-1]
    qkv = xn @ w_qkv + b_qkv
    nq, nk = N_HEADS * HEAD_DIM, N_KV_HEADS * HEAD_DIM
    q = rms_norm(qkv[..., :nq].reshape(lead + (N_HEADS, HEAD_DIM)), q_g)
    k = rms_norm(qkv[..., nq:nq + nk].reshape(lead + (N_KV_HEADS, HEAD_DIM)), k_g)
    v = qkv[..., nq + nk:].reshape(lead + (N_KV_HEADS, HEAD_DIM))
    q = q.reshape(lead + (N_KV_HEADS, Q_PER_KV, HEAD_DIM))
    return q, k, v


def sink_attend(q, k, v, logit_bias, sinks):
    s = jnp.einsum('...qkgd,...skd->...kgqs', q.astype(jnp.float32), k.astype(jnp.float32))
    s = s * (HEAD_DIM ** -0.5) + logit_bias
    sink = jnp.broadcast_to(sinks.astype(jnp.float32).reshape(N_KV_HEADS, Q_PER_KV, 1, 1),
                            s.shape[:-1] + (1,))
    p = jax.nn.softmax(jnp.concatenate([s, sink], axis=-1), axis=-1)[..., :-1]
    return jnp.einsum('...kgqs,...skd->...qkgd', p, v.astype(jnp.float32))


def swa_prompt(xn, w_qkv, b_qkv, q_g, k_g, sinks, w_o, b_o, rel_bias):
    bsz, s_len = xn.shape[0], xn.shape[1]
    nb = s_len // WINDOW
    q, k, v = swa_qkv(xn, w_qkv, b_qkv, q_g, k_g)
    qb = q.reshape(bsz, nb, WINDOW, N_KV_HEADS, Q_PER_KV, HEAD_DIM)

    def band(t):
        tp = jnp.pad(t, ((0, 0), (WINDOW, 0), (0, 0), (0, 0)))
        tp = tp.reshape(bsz, nb + 1, WINDOW, N_KV_HEADS, HEAD_DIM)
        return jnp.concatenate([tp[:, :-1], tp[:, 1:]], axis=2)

    kb, vb = band(k), band(v)
    i = jnp.arange(WINDOW)
    j = jnp.arange(2 * WINDOW)
    dist = WINDOW + i[:, None] - j[None, :]
    bias = window_bias(dist, rel_bias)
    key_pos = (jnp.arange(nb)[:, None] - 1) * WINDOW + j[None, :]
    bias = jnp.where((key_pos >= 0)[:, None, None, None, :], bias[None], NEG_INF)
    o = sink_attend(qb, kb, vb, bias, sinks)
    y = o.reshape(bsz, s_len, N_HEADS * HEAD_DIM).astype(xn.dtype) @ w_o + b_o
    keep = min(WINDOW, s_len)
    return y, k[:, s_len - keep:], v[:, s_len - keep:]


def swa_sample(xn, cache_k, cache_v, w_qkv, b_qkv, q_g, k_g, sinks, w_o, b_o, rel_bias):
    bsz, t_len = xn.shape[0], xn.shape[1]
    cw = cache_k.shape[1]
    q, k, v = swa_qkv(xn, w_qkv, b_qkv, q_g, k_g)
    k_all = jnp.concatenate([cache_k.astype(k.dtype), k], axis=1)
    v_all = jnp.concatenate([cache_v.astype(v.dtype), v], axis=1)
    q_pos = PAST_LEN + jnp.arange(t_len)
    k_pos = jnp.concatenate([PAST_LEN - cw + jnp.arange(cw), q_pos])
    bias = window_bias(q_pos[:, None] - k_pos[None, :], rel_bias)
    o = sink_attend(q, k_all, v_all, bias, sinks)
    y = o.reshape(bsz, t_len, N_HEADS * HEAD_DIM).astype(xn.dtype) @ w_o + b_o
    return y, k_all[:, -cw:], v_all[:, -cw:]


def moe(xn, w_router, b_router, w1, b1, w2, b2):
    shape = xn.shape
    xt = xn.reshape(-1, shape[-1])
    logits = (xt @ w_router + b_router).astype(jnp.float32)
    top_v, top_i = lax.top_k(logits, TOP_K)
    gates = jax.nn.softmax(top_v, axis=-1)
    comb = jnp.einsum('tk,tke->te', gates,
                      jax.nn.one_hot(top_i, N_EXPERTS, dtype=jnp.float32)).astype(xt.dtype)
    out = jnp.zeros_like(xt)
    for e in range(N_EXPERTS):
        h = xt @ w1[e] + b1[e]
        g = jnp.minimum(h[:, 0::2], SWIGLU_LIMIT)
        up = jnp.clip(h[:, 1::2], -SWIGLU_LIMIT, SWIGLU_LIMIT)
        a = (up + 1.0) * (g * jax.nn.sigmoid(SWIGLU_ALPHA * g))
        out = out + comb[:, e:e + 1] * (a @ w2[e] + b2[e])
    return out.reshape(shape)


def setup_inputs(seed: int = 0) -> dict:
    key = jax.random.key(seed)
    ks = iter(jax.random.split(key, 40))
    f32 = jnp.float32
    nrm = lambda shape, scale: scale * jax.random.normal(next(ks), shape, f32)
    cw = min(WINDOW, PAST_LEN)
    d_qkv = (N_HEADS + 2 * N_KV_HEADS) * HEAD_DIM
    return {
        "x_prompt": nrm((BATCH, SEQ, D_MODEL), 1.0),
        "x_sample": nrm((DEC_BATCH, DEC_SEQ, D_MODEL), 1.0),
        "cache_k": nrm((N_SWA_LAYERS, DEC_BATCH, cw, N_KV_HEADS, HEAD_DIM), 1.0),
        "cache_v": nrm((N_SWA_LAYERS, DEC_BATCH, cw, N_KV_HEADS, HEAD_DIM), 1.0),
        "norm_mix_g": 1.0 + nrm((DEPTH, D_MODEL), 0.1),
        "norm_ffn_g": 1.0 + nrm((DEPTH, D_MODEL), 0.1),
        "a_w_in": nrm((N_CHUNK_LAYERS, D_MODEL, 2 * D_HALF_A), D_MODEL ** -0.5),
        "a_b_in": nrm((N_CHUNK_LAYERS, 2 * D_HALF_A), 0.02),
        "a_ln_g": 1.0 + nrm((N_CHUNK_LAYERS, D_HALF_A), 0.1),
        "a_ln_b": nrm((N_CHUNK_LAYERS, D_HALF_A), 0.02),
        "a_w_s": nrm((N_CHUNK_LAYERS, N_GROUPS_A, CHUNK, CHUNK), CHUNK ** -0.5),
        "a_b_s": 1.0 + nrm((N_CHUNK_LAYERS, N_GROUPS_A, CHUNK), 0.1),
        "a_w_out": nrm((N_CHUNK_LAYERS, D_HALF_A, D_MODEL), 0.5 * D_HALF_A ** -0.5),
        "a_b_out": nrm((N_CHUNK_LAYERS, D_MODEL), 0.02),
        "s_w_qkv": nrm((N_SWA_LAYERS, D_MODEL, d_qkv), D_MODEL ** -0.5),
        "s_b_qkv": nrm((N_SWA_LAYERS, d_qkv), 0.02),
        "s_q_norm_g": 1.0 + nrm((N_SWA_LAYERS, HEAD_DIM), 0.1),
        "s_k_norm_g": 1.0 + nrm((N_SWA_LAYERS, HEAD_DIM), 0.1),
        "s_sinks": nrm((N_SWA_LAYERS, N_HEADS), 0.5),
        "s_w_o": nrm((N_SWA_LAYERS, N_HEADS * HEAD_DIM, D_MODEL), 0.5 * (N_HEADS * HEAD_DIM) ** -0.5),
        "s_b_o": nrm((N_SWA_LAYERS, D_MODEL), 0.02),
        "rel_bias": nrm((NUM_BUCKETS, N_HEADS), 0.5),
        "m_w_router": nrm((DEPTH, D_MODEL, N_EXPERTS), D_MODEL ** -0.5),
        "m_b_router": nrm((DEPTH, N_EXPERTS), 0.01),
        "m_w1": nrm((DEPTH, N_EXPERTS, D_MODEL, 2 * D_FF_E), D_MODEL ** -0.5),
        "m_b1": nrm((DEPTH, N_EXPERTS, 2 * D_FF_E), 0.02),
        "m_w2": nrm((DEPTH, N_EXPERTS, D_FF_E, D_MODEL), 0.5 * D_FF_E ** -0.5),
        "m_b2": nrm((DEPTH, N_EXPERTS, D_MODEL), 0.02),
    }


def reference(x_prompt, x_sample, cache_k, cache_v, norm_mix_g, norm_ffn_g,
              a_w_in, a_b_in, a_ln_g, a_ln_b, a_w_s, a_b_s, a_w_out, a_b_out,
              s_w_qkv, s_b_qkv, s_q_norm_g, s_k_norm_g, s_sinks, s_w_o, s_b_o, rel_bias,
              m_w_router, m_b_router, m_w1, m_b1, m_w2, m_b2):
    yp, ys = x_prompt, x_sample
    new_kp, new_vp, new_ks, new_vs, new_va = [], [], [], [], []
    for i in range(DEPTH):
        j = i // N_MIXERS
        xp_n = rms_norm(yp, norm_mix_g[i])
        xs_n = rms_norm(ys, norm_mix_g[i])
        if i % N_MIXERS == 0:
            pa = (a_w_in[j], a_b_in[j], a_ln_g[j], a_ln_b[j], a_w_s[j], a_b_s[j], a_w_out[j], a_b_out[j])
            hp, _ = chunk_sgu(xp_n, *pa)
            hs, vs = chunk_sgu(xs_n, *pa)
            new_va.append(vs)
        else:
            pb = (s_w_qkv[j], s_b_qkv[j], s_q_norm_g[j], s_k_norm_g[j], s_sinks[j], s_w_o[j], s_b_o[j], rel_bias)
            hp, kp, vp = swa_prompt(xp_n, *pb)
            hs, ks_, vs_ = swa_sample(xs_n, cache_k[j], cache_v[j], *pb)
            new_kp.append(kp)
            new_vp.append(vp)
            new_ks.append(ks_)
            new_vs.append(vs_)
        yp = yp + hp
        ys = ys + hs
        pm = (m_w_router[i], m_b_router[i], m_w1[i], m_b1[i], m_w2[i], m_b2[i])
        yp = yp + moe(rms_norm(yp, norm_ffn_g[i]), *pm)
        ys = ys + moe(rms_norm(ys, norm_ffn_g[i]), *pm)
    return (yp, ys, jnp.stack(new_kp), jnp.stack(new_vp), jnp.stack(new_ks), jnp.stack(new_vs), jnp.stack(new_va))
```

```python
import functools
import math

import jax
import jax.numpy as jnp
from jax import lax
from jax.experimental import pallas as pl
from jax.experimental.pallas import tpu as pltpu

EPS = 1e-6
NEG_INF = -1e30
CHUNK = 128
WINDOW = 128
NUM_BUCKETS = 32
MAX_DISTANCE = 128
TOP_K = 4
SWIGLU_LIMIT = 7.0
SWIGLU_ALPHA = 1.702

LANES = 128
SUBLANES = 8
VMEM_LIMIT_BYTES = 56 << 20

ROW_BLOCK = 256
ROUTE_BLOCK = 512
EXPERT_TILE = 256
DISPATCH_BLOCK = 512
COMBINE_BLOCK = 256
SAMPLE_SEQ_BLOCK = 32

F32 = jnp.float32
BF16 = jnp.bfloat16


def _params(*sem):
    return pltpu.CompilerParams(dimension_semantics=sem, vmem_limit_bytes=VMEM_LIMIT_BYTES)


def _rms(x, g):
    return x * lax.rsqrt(jnp.mean(x * x, axis=-1, keepdims=True) + EPS) * g


def _dot(a, b):
    return jnp.dot(a, b, preferred_element_type=F32)


def _dot_nt(a, b):
    return lax.dot_general(a, b, (((1,), (1,)), ((), ())), preferred_element_type=F32)


def _sgu_body(n_sample_blocks, n_groups, y_ref, g_ref, win_ref, bin_ref, lng_ref, lnb_ref,
              wsp_ref, bsp_ref, wout_ref, bout_ref, y2_ref, v_ref):
    i = pl.program_id(0)
    x = y_ref[...]
    xn = _rms(x, g_ref[...]).astype(BF16)
    z = _dot(xn, win_ref[...]) + bin_ref[...]
    z = 0.5 * z * (1.0 + lax.erf(z * (1.0 / math.sqrt(2.0))))
    half = z.shape[1] // 2
    u, v = z[:, :half], z[:, half:]
    mu = jnp.mean(v, axis=-1, keepdims=True)
    vc = v - mu
    vn = vc * lax.rsqrt(jnp.mean(vc * vc, axis=-1, keepdims=True) + EPS) * lng_ref[...] + lnb_ref[...]

    @pl.when(i < n_sample_blocks)
    def _():
        v_ref[...] = vn

    vb = vn.astype(BF16)
    dg = half // n_groups
    rows = []
    for c in range(x.shape[0] // CHUNK):
        r0 = c * CHUNK
        cols = []
        for g in range(n_groups):
            gate = _dot(wsp_ref[0, g], vb[r0:r0 + CHUNK, g * dg:(g + 1) * dg])
            gate = gate + bsp_ref[0, :, g * dg:(g + 1) * dg]
            cols.append(u[r0:r0 + CHUNK, g * dg:(g + 1) * dg] * gate)
        rows.append(jnp.concatenate(cols, axis=1))
    out = jnp.concatenate(rows, axis=0).astype(BF16)
    y2_ref[...] = x + _dot(out, wout_ref[...]) + bout_ref[...]


def _sgu_call(y, n_sample_rows, g, w_in, b_in, ln_g, ln_b, wsp, bsp, w_out, b_out):
    t, d = y.shape
    rb = ROW_BLOCK
    nsb = n_sample_rows // rb
    n_groups = wsp.shape[1]
    half = w_out.shape[0]
    const = lambda shape: pl.BlockSpec(shape, lambda i: (0,) * len(shape))
    which = lambda i: jnp.where(i < nsb, 0, 1)
    return pl.pallas_call(
        functools.partial(_sgu_body, nsb, n_groups),
        grid=(t // rb,),
        in_specs=[
            pl.BlockSpec((rb, d), lambda i: (i, 0)),
            const((1, d)), const(w_in.shape), const((1, 2 * half)), const((1, half)), const((1, half)),
            pl.BlockSpec((1, n_groups, CHUNK, CHUNK), lambda i: (which(i), 0, 0, 0)),
            pl.BlockSpec((1, CHUNK, half), lambda i: (which(i), 0, 0)),
            const(w_out.shape), const((1, d)),
        ],
        out_specs=[
            pl.BlockSpec((rb, d), lambda i: (i, 0)),
            pl.BlockSpec((rb, half), lambda i: (jnp.minimum(i, nsb - 1), 0)),
        ],
        out_shape=[jax.ShapeDtypeStruct((t, d), F32), jax.ShapeDtypeStruct((n_sample_rows, half), F32)],
        compiler_params=_params("arbitrary"),
        name="sgu",
    )(y, g, w_in, b_in, ln_g, ln_b, wsp, bsp, w_out, b_out)


def _head_mean_square(z, ones_bd, head_dim):
    sq = z * z
    tiles = z.shape[1] // LANES
    st = jnp.concatenate([sq[:, c * LANES:(c + 1) * LANES] for c in range(tiles)], axis=0)
    hi = st.astype(BF16)
    lo = (st - hi.astype(F32)).astype(BF16)
    ms = (_dot(hi, ones_bd) + _dot(lo, ones_bd)) * (1.0 / head_dim)
    r = z.shape[0]
    return jnp.concatenate([ms[c * r:(c + 1) * r] for c in range(tiles)], axis=1)


def _qkv(x, g_ref, wqkv_ref, bqkv_ref, qg_ref, kg_ref, ones_ref, n_q, n_k, head_dim):
    xn = _rms(x, g_ref[...]).astype(BF16)
    qkv = _dot(xn, wqkv_ref[...]) + bqkv_ref[...]
    q, k, v = qkv[:, :n_q], qkv[:, n_q:n_q + n_k], qkv[:, n_q + n_k:]
    ones_bd = ones_ref[...]
    q = q * lax.rsqrt(_head_mean_square(q, ones_bd, head_dim) + EPS) * qg_ref[...]
    k = k * lax.rsqrt(_head_mean_square(k, ones_bd, head_dim) + EPS) * kg_ref[...]
    return q, k, v


def _swa_prompt_body(n_kv, q_per_kv, head_dim, y_ref, g_ref, wqkv_ref, bqkv_ref, qg_ref, kg_ref,
                     ones_ref, bias_ref, sink_ref, wo_ref, bo_ref, y2_ref, ko_ref, vo_ref,
                     kprev, vprev):
    n = pl.program_id(1)
    n_q, n_k = n_kv * q_per_kv * head_dim, n_kv * head_dim
    x = y_ref[...]
    q, k, v = _qkv(x, g_ref, wqkv_ref, bqkv_ref, qg_ref, kg_ref, ones_ref, n_q, n_k, head_dim)
    qb = (q * (head_dim ** -0.5)).astype(BF16)
    kb, vb = k.astype(BF16), v.astype(BF16)

    @pl.when(n == 0)
    def _():
        kprev[...] = jnp.zeros_like(kprev)
        vprev[...] = jnp.zeros_like(vprev)

    w = WINDOW
    col = lax.broadcasted_iota(jnp.int32, (q_per_kv * w, 2 * w), 1)
    no_prev = jnp.logical_and(n == 0, col < w)
    att_rows = []
    for c in range(x.shape[0] // w):
        r0 = c * w
        k_prev = kprev[...] if c == 0 else kb[r0 - w:r0]
        v_prev = vprev[...] if c == 0 else vb[r0 - w:r0]
        k2 = jnp.concatenate([k_prev, kb[r0:r0 + w]], axis=0)
        v2 = jnp.concatenate([v_prev, vb[r0:r0 + w]], axis=0)
        heads = []
        for j in range(n_kv):
            qj = jnp.concatenate(
                [qb[r0:r0 + w, (j * q_per_kv + g) * head_dim:(j * q_per_kv + g + 1) * head_dim]
                 for g in range(q_per_kv)], axis=0)
            bias = bias_ref[j]
            if c == 0:
                bias = jnp.where(no_prev, NEG_INF, bias)
            s = _dot_nt(qj, k2[:, j * head_dim:(j + 1) * head_dim]) + bias
            sink = sink_ref[j]
            m = jnp.maximum(jnp.max(s, axis=1, keepdims=True), sink)
            p = jnp.exp(s - m)
            den = jnp.sum(p, axis=1, keepdims=True) + jnp.exp(sink - m)
            o = _dot(p.astype(BF16), v2[:, j * head_dim:(j + 1) * head_dim]) / den
            heads.extend(o[g * w:(g + 1) * w] for g in range(q_per_kv))
        att_rows.append(jnp.concatenate(heads, axis=1))
    att = jnp.concatenate(att_rows, axis=0).astype(BF16)
    y2_ref[...] = x + _dot(att, wo_ref[...]) + bo_ref[...]
    last = x.shape[0] - w
    kprev[...] = kb[last:]
    vprev[...] = vb[last:]

    @pl.when(n == pl.num_programs(1) - 1)
    def _():
        ko_ref[0] = k[last:]
        vo_ref[0] = v[last:]


def _swa_prompt_call(y, n_sample_rows, batch, seq, g, w_qkv, b_qkv, qg, kg, ones_bd, bias, sink,
                     w_o, b_o, n_kv, q_per_kv, head_dim):
    t, d = y.shape
    rb = ROW_BLOCK
    nb = seq // rb
    base = n_sample_rows // rb
    n_k = n_kv * head_dim
    const = lambda shape: pl.BlockSpec(shape, lambda b, n: (0,) * len(shape))
    row_spec = pl.BlockSpec((rb, d), lambda b, n: (base + b * nb + n, 0))
    kv_spec = pl.BlockSpec((1, WINDOW, n_k), lambda b, n: (b, 0, 0))
    return pl.pallas_call(
        functools.partial(_swa_prompt_body, n_kv, q_per_kv, head_dim),
        grid=(batch, nb),
        in_specs=[row_spec, const((1, d)), const(w_qkv.shape), const(b_qkv.shape), const(qg.shape),
                  const(kg.shape), const(ones_bd.shape), const(bias.shape), const(sink.shape),
                  const(w_o.shape), const((1, d))],
        out_specs=[row_spec, kv_spec, kv_spec],
        out_shape=[jax.ShapeDtypeStruct((t, d), F32),
                   jax.ShapeDtypeStruct((batch, WINDOW, n_k), F32),
                   jax.ShapeDtypeStruct((batch, WINDOW, n_k), F32)],
        scratch_shapes=[pltpu.VMEM((WINDOW, n_k), BF16), pltpu.VMEM((WINDOW, n_k), BF16)],
        input_output_aliases={0: 0},
        compiler_params=_params("arbitrary", "arbitrary"),
        name="swa_prompt",
    )(y, g, w_qkv, b_qkv, qg, kg, ones_bd, bias, sink, w_o, b_o)


def _swa_sample_body(n_kv, q_per_kv, head_dim, dec_seq, y_ref, ck_ref, cv_ref, g_ref, wqkv_ref,
                     bqkv_ref, qg_ref, kg_ref, ones_ref, biasc_ref, biasn_ref, sink_ref, wo_ref,
                     bo_ref, y2_ref, ko_ref, vo_ref, q_scr, k_scr, v_scr, att_scr):
    n_q, n_k = n_kv * q_per_kv * head_dim, n_kv * head_dim
    x = y_ref[...]
    q, k, v = _qkv(x, g_ref, wqkv_ref, bqkv_ref, qg_ref, kg_ref, ones_ref, n_q, n_k, head_dim)
    q_scr[...] = q * (head_dim ** -0.5)
    k_scr[...] = k
    v_scr[...] = v
    cw = ck_ref.shape[1]
    pad = jnp.zeros((SUBLANES - dec_seq, n_k), F32)

    def one_seq(s, qs, k_new, v_new):
        ck, cv = ck_ref[s], cv_ref[s]
        ko_ref[s, 0:cw - dec_seq, :] = ck[dec_seq:]
        ko_ref[s, cw - dec_seq:cw, :] = k_new
        vo_ref[s, 0:cw - dec_seq, :] = cv[dec_seq:]
        vo_ref[s, cw - dec_seq:cw, :] = v_new
        qs = qs.astype(BF16)
        ckb, cvb = ck.astype(BF16), cv.astype(BF16)
        kn8 = jnp.concatenate([k_new, pad], axis=0).astype(BF16)
        vn8 = jnp.concatenate([v_new, pad], axis=0).astype(BF16)
        outs = []
        for j in range(n_kv):
            lo, hi = j * head_dim, (j + 1) * head_dim
            qj = jnp.concatenate(
                [qs[:, (j * q_per_kv + g) * head_dim:(j * q_per_kv + g + 1) * head_dim]
                 for g in range(q_per_kv)], axis=0)
            s_c = _dot_nt(qj, ckb[:, lo:hi]) + biasc_ref[j]
            s_n = _dot_nt(qj, kn8[:, lo:hi]) + biasn_ref[j]
            sink = sink_ref[j]
            m = jnp.maximum(jnp.maximum(jnp.max(s_c, axis=1, keepdims=True),
                                        jnp.max(s_n, axis=1, keepdims=True)), sink)
            p_c, p_n = jnp.exp(s_c - m), jnp.exp(s_n - m)
            den = (jnp.sum(p_c, axis=1, keepdims=True) + jnp.sum(p_n, axis=1, keepdims=True)
                   + jnp.exp(sink - m))
            o = (_dot(p_c.astype(BF16), cvb[:, lo:hi]) + _dot(p_n.astype(BF16), vn8[:, lo:hi])) / den
            outs.extend(o[g * dec_seq:(g + 1) * dec_seq] for g in range(q_per_kv))
        return jnp.concatenate(outs, axis=1)

    per_group = SUBLANES // dec_seq

    def per_group_of_rows(p, carry):
        r = pl.multiple_of(p * SUBLANES, SUBLANES)
        q8, k8, v8 = q_scr[pl.ds(r, SUBLANES), :], k_scr[pl.ds(r, SUBLANES), :], v_scr[pl.ds(r, SUBLANES), :]
        outs = []
        for u in range(per_group):
            rows = slice(u * dec_seq, (u + 1) * dec_seq)
            outs.append(one_seq(p * per_group + u, q8[rows], k8[rows], v8[rows]))
        att_scr[pl.ds(r, SUBLANES), :] = jnp.concatenate(outs, axis=0)
        return carry

    lax.fori_loop(0, ck_ref.shape[0] // per_group, per_group_of_rows, 0)
    y2_ref[...] = x + _dot(att_scr[...].astype(BF16), wo_ref[...]) + bo_ref[...]


def _swa_sample_call(y, ck, cv, dec_seq, g, w_qkv, b_qkv, qg, kg, ones_bd, bias_c, bias_n, sink,
                     w_o, b_o, n_kv, q_per_kv, head_dim):
    t, d = y.shape
    dec_batch, cw, n_k = ck.shape
    sb = min(SAMPLE_SEQ_BLOCK, dec_batch)
    rb = sb * dec_seq
    n_q = n_kv * q_per_kv * head_dim
    const = lambda shape: pl.BlockSpec(shape, lambda i: (0,) * len(shape))
    row_spec = pl.BlockSpec((rb, d), lambda i: (i, 0))
    cache_spec = pl.BlockSpec((sb, cw, n_k), lambda i: (i, 0, 0))
    return pl.pallas_call(
        functools.partial(_swa_sample_body, n_kv, q_per_kv, head_dim, dec_seq),
        grid=(dec_batch // sb,),
        in_specs=[row_spec, cache_spec, cache_spec, const((1, d)), const(w_qkv.shape),
                  const(b_qkv.shape), const(qg.shape), const(kg.shape), const(ones_bd.shape),
                  const(bias_c.shape), const(bias_n.shape), const(sink.shape), const(w_o.shape),
                  const((1, d))],
        out_specs=[row_spec, cache_spec, cache_spec],
        out_shape=[jax.ShapeDtypeStruct((t, d), F32),
                   jax.ShapeDtypeStruct(ck.shape, F32), jax.ShapeDtypeStruct(cv.shape, F32)],
        scratch_shapes=[pltpu.VMEM((rb, n_q), F32), pltpu.VMEM((rb, n_k), F32),
                        pltpu.VMEM((rb, n_k), F32), pltpu.VMEM((rb, n_q), F32)],
        input_output_aliases={0: 0},
        compiler_params=_params("arbitrary"),
        name="swa_sample",
    )(y, ck, cv, g, w_qkv, b_qkv, qg, kg, ones_bd, bias_c, bias_n, sink, w_o, b_o)


def _route_body(y_ref, g_ref, wr_ref, br_ref, xn_ref, ids_ref, gates_ref, rank_ref, tot_ref, carry):
    i = pl.program_id(0)

    @pl.when(i == 0)
    def _():
        carry[...] = jnp.zeros_like(carry)

    xn = _rms(y_ref[...], g_ref[...])
    xn_ref[...] = xn
    logits = lax.dot_general(wr_ref[...], xn, (((1,), (1,)), ((), ())),
                             precision=lax.Precision.HIGHEST, preferred_element_type=F32)
    logits = logits + br_ref[:, 0:1]
    n_exp, tb = logits.shape
    e_iota = lax.broadcasted_iota(jnp.int32, (n_exp, tb), 0)
    work = logits
    vals, ids, hots = [], [], []
    for _ in range(TOP_K):
        m = jnp.max(work, axis=0, keepdims=True)
        idx = jnp.min(jnp.where(work == m, e_iota, n_exp), axis=0, keepdims=True)
        hot = e_iota == idx
        vals.append(m)
        ids.append(idx)
        hots.append(hot)
        work = jnp.where(hot, -jnp.inf, work)
    ex = [jnp.exp(v - vals[0]) for v in vals]
    den = ex[0] + ex[1] + ex[2] + ex[3]
    ids_ref[...] = jnp.concatenate(ids, axis=0)
    gates_ref[...] = jnp.concatenate([e / den for e in ex], axis=0)

    hot_all = jnp.zeros((n_exp, tb), F32)
    for hot in hots:
        hot_all = hot_all + hot.astype(F32)
    before = (lax.broadcasted_iota(jnp.int32, (tb, tb), 0)
              < lax.broadcasted_iota(jnp.int32, (tb, tb), 1)).astype(BF16)
    count = _dot(hot_all.astype(BF16), before) + carry[:, 0:1]
    rank_ref[...] = jnp.concatenate(
        [jnp.sum(jnp.where(hot, count, 0.0), axis=0, keepdims=True) for hot in hots],
        axis=0).astype(jnp.int32)
    total = carry[...] + jnp.sum(hot_all, axis=1, keepdims=True)
    carry[...] = total
    tot_ref[...] = total.astype(jnp.int32)


def _route_call(y, g, w_r_t, b_r):
    t, d = y.shape
    n_exp = w_r_t.shape[0]
    tb = ROUTE_BLOCK
    const = lambda shape: pl.BlockSpec(shape, lambda i: (0,) * len(shape))
    tok = lambda dtype: jax.ShapeDtypeStruct((TOP_K, t), dtype)
    tok_spec = pl.BlockSpec((TOP_K, tb), lambda i: (0, i))
    return pl.pallas_call(
        _route_body,
        grid=(t // tb,),
        in_specs=[pl.BlockSpec((tb, d), lambda i: (i, 0)), const((1, d)), const((n_exp, d)),
                  const((n_exp, LANES))],
        out_specs=[pl.BlockSpec((tb, d), lambda i: (i, 0)), tok_spec, tok_spec, tok_spec,
                   const((n_exp, LANES))],
        out_shape=[jax.ShapeDtypeStruct((t, d), F32), tok(jnp.int32), tok(F32), tok(jnp.int32),
                   jax.ShapeDtypeStruct((n_exp, LANES), jnp.int32)],
        scratch_shapes=[pltpu.VMEM((n_exp, LANES), F32)],
        compiler_params=_params("arbitrary"),
        name="moe_route",
    )(y, g, w_r_t, b_r)


def _row_copy(src, src_row, dst, dst_row, sem):
    return pltpu.make_async_copy(src.at[pl.ds(src_row, 1)], dst.at[pl.ds(dst_row, 1)], sem)


def _dispatch_body(tb, tile, ztile_ref, pos_ref, xn_hbm, xs_hbm, zeros, sem, zsem):
    i = pl.program_id(0)
    nsteps = pl.num_programs(0)

    @pl.when(i == 0)
    def _():
        zeros[...] = jnp.zeros_like(zeros)

        def fill(e):
            start = pl.multiple_of(ztile_ref[e], tile)
            return pltpu.make_async_copy(zeros, xs_hbm.at[pl.ds(start, tile)], zsem)

        for e in range(ztile_ref.shape[0]):
            @pl.when(ztile_ref[e] >= 0)
            def _():
                fill(e).start()
        for e in range(ztile_ref.shape[0]):
            @pl.when(ztile_ref[e] >= 0)
            def _():
                fill(e).wait()

    def issue(t, carry):
        for k in range(TOP_K):
            _row_copy(xn_hbm, i * tb + t, xs_hbm, pos_ref[k, t], sem.at[i % 2]).start()
        return carry

    lax.fori_loop(0, tb, issue, 0, unroll=8)

    def drain(slot):
        for k in range(TOP_K):
            pltpu.make_async_copy(xn_hbm.at[pl.ds(0, tb)], xs_hbm.at[pl.ds(0, tb)], sem.at[slot]).wait()

    @pl.when(i > 0)
    def _():
        drain((i + 1) % 2)

    @pl.when(i == nsteps - 1)
    def _():
        drain(i % 2)


def _dispatch_call(pos, ztile, xn, n_rows_padded):
    t, d = xn.shape
    tb = DISPATCH_BLOCK
    tile = EXPERT_TILE
    return pl.pallas_call(
        functools.partial(_dispatch_body, tb, tile),
        grid_spec=pltpu.PrefetchScalarGridSpec(
            num_scalar_prefetch=1,
            grid=(t // tb,),
            in_specs=[pl.BlockSpec((TOP_K, tb), lambda i, z: (0, i), memory_space=pltpu.SMEM),
                      pl.BlockSpec(memory_space=pl.ANY)],
            out_specs=pl.BlockSpec(memory_space=pl.ANY),
            scratch_shapes=[pltpu.VMEM((tile, d), F32), pltpu.SemaphoreType.DMA((2,)),
                            pltpu.SemaphoreType.DMA]),
        out_shape=jax.ShapeDtypeStruct((n_rows_padded, d), F32),
        compiler_params=_params("arbitrary"),
        name="moe_dispatch",
    )(ztile, pos, xn)


def _expert_body(te_ref, nt_ref, xs_ref, w1_ref, b1_ref, w2_ref, b2_ref, sel_ref, ys_ref, w1p, w2b):
    i = pl.program_id(0)
    prev = te_ref[jnp.maximum(i - 1, 0)]
    fresh = jnp.logical_or(i == 0, te_ref[i] != prev)
    live = i < nt_ref[0]
    d_ff2 = w1p.shape[1]
    pair = 2 * LANES

    @pl.when(jnp.logical_and(live, fresh))
    def _():
        sel = sel_ref[...]
        for c in range(d_ff2 // pair):
            blk = w1_ref[0, :, c * pair:(c + 1) * pair].astype(BF16)
            w1p[:, c * pair:(c + 1) * pair] = _dot(blk, sel).astype(BF16)
        w2b[...] = w2_ref[0].astype(BF16)

    @pl.when(live)
    def _():
        x = xs_ref[...].astype(BF16)
        h = _dot(x, w1p[...]) + b1_ref[0]
        acts = []
        for c in range(d_ff2 // pair):
            gate = jnp.minimum(h[:, c * pair:c * pair + LANES], SWIGLU_LIMIT)
            up = jnp.clip(h[:, c * pair + LANES:(c + 1) * pair], -SWIGLU_LIMIT, SWIGLU_LIMIT)
            acts.append((up + 1.0) * (gate * (1.0 / (1.0 + jnp.exp(-SWIGLU_ALPHA * gate)))))
        a = jnp.concatenate(acts, axis=1).astype(BF16)
        ys_ref[...] = _dot(a, w2b[...]) + b2_ref[0]

    @pl.when(jnp.logical_not(live))
    def _():
        ys_ref[...] = jnp.zeros_like(ys_ref)


def _expert_call(tile_expert, n_tiles, xs, w1, b1p, w2, b2, sel):
    r, d = xs.shape
    tile = EXPERT_TILE
    n_exp, _, d_ff2 = w1.shape
    d_ff = w2.shape[1]
    row = lambda i, te, nt: (i, 0)
    exp3 = lambda i, te, nt: (te[i], 0, 0)
    return pl.pallas_call(
        _expert_body,
        grid_spec=pltpu.PrefetchScalarGridSpec(
            num_scalar_prefetch=2,
            grid=(r // tile,),
            in_specs=[pl.BlockSpec((tile, d), row),
                      pl.BlockSpec((1, d, d_ff2), exp3), pl.BlockSpec((1, 1, d_ff2), exp3),
                      pl.BlockSpec((1, d_ff, d), exp3), pl.BlockSpec((1, 1, d), exp3),
                      pl.BlockSpec(sel.shape, lambda i, te, nt: (0, 0))],
            out_specs=pl.BlockSpec((tile, d), row),
            scratch_shapes=[pltpu.VMEM((d, d_ff2), BF16), pltpu.VMEM((d_ff, d), BF16)]),
        out_shape=jax.ShapeDtypeStruct((r, d), F32),
        compiler_params=_params("arbitrary"),
        name="moe_expert",
    )(tile_expert, n_tiles, xs, w1, b1p, w2, b2, sel)


def _combine_body(tb, pos_ref, posn_ref, gates_ref, y_ref, ys_hbm, out_ref, buf, sem):
    i = pl.program_id(0)
    nsteps = pl.num_programs(0)
    slot = i % 2

    def issue(table, dst_slot):
        def one(t, carry):
            for k in range(TOP_K):
                pltpu.make_async_copy(ys_hbm.at[pl.ds(table[k, t], 1)],
                                      buf.at[dst_slot, k, pl.ds(t, 1)], sem.at[dst_slot]).start()
            return carry
        lax.fori_loop(0, tb, one, 0, unroll=8)

    @pl.when(i == 0)
    def _():
        issue(pos_ref, 0)

    @pl.when(i + 1 < nsteps)
    def _():
        issue(posn_ref, 1 - slot)

    for k in range(TOP_K):
        pltpu.make_async_copy(ys_hbm.at[pl.ds(0, tb)], buf.at[slot, k], sem.at[slot]).wait()
    acc = y_ref[...]
    for k in range(TOP_K):
        acc = acc + gates_ref[:, k:k + 1] * buf[slot, k]
    out_ref[...] = acc


def _combine_call(pos, gates_t, y, ys):
    t, d = y.shape
    tb = COMBINE_BLOCK
    nb = t // tb
    smem = lambda fn: pl.BlockSpec((TOP_K, tb), fn, memory_space=pltpu.SMEM)
    return pl.pallas_call(
        functools.partial(_combine_body, tb),
        grid=(nb,),
        in_specs=[smem(lambda i: (0, i)), smem(lambda i: (0, jnp.minimum(i + 1, nb - 1))),
                  pl.BlockSpec((tb, TOP_K), lambda i: (i, 0)),
                  pl.BlockSpec((tb, d), lambda i: (i, 0)),
                  pl.BlockSpec(memory_space=pl.ANY)],
        out_specs=pl.BlockSpec((tb, d), lambda i: (i, 0)),
        out_shape=jax.ShapeDtypeStruct((t, d), F32),
        scratch_shapes=[pltpu.VMEM((2, TOP_K, tb, d), F32), pltpu.SemaphoreType.DMA((2,))],
        compiler_params=_params("arbitrary"),
        name="moe_combine",
    )(pos, pos, gates_t, y, ys)


def _moe(y, g, w_r, b_r, w1, b1, w2, b2, sel):
    t, d = y.shape
    n_exp = w_r.shape[1]
    tile = EXPERT_TILE
    max_tiles = (TOP_K * t) // tile + n_exp
    b_r_l = jnp.broadcast_to(b_r[:, None], (n_exp, LANES))
    xn, ids, gates, rank, totals = _route_call(y, g, w_r.T, b_r_l)

    n_e = totals[:, 0]
    tiles_e = (n_e + tile - 1) // tile
    tile_end = jnp.cumsum(tiles_e)
    tile_start = tile_end - tiles_e
    n_tiles = tile_end[-1:]
    pos = jnp.take(tile_start * tile, ids, axis=0) + rank
    tile_ids = jnp.minimum(jnp.arange(max_tiles, dtype=jnp.int32), n_tiles[0] - 1)
    tile_expert = jnp.searchsorted(tile_end, tile_ids, side="right").astype(jnp.int32)
    spare = n_tiles[0] + jnp.arange(n_exp, dtype=jnp.int32)
    ztile = jnp.concatenate([jnp.where(n_e % tile != 0, (tile_end - 1) * tile, -1),
                             jnp.where(spare < max_tiles, spare * tile, -1)]).astype(jnp.int32)

    xs = _dispatch_call(pos, ztile, xn, max_tiles * tile)
    d_ff2 = w1.shape[2]
    b1p = b1.reshape(n_exp, d_ff2 // (2 * LANES), LANES, 2).swapaxes(2, 3).reshape(n_exp, 1, d_ff2)
    ys = _expert_call(tile_expert, n_tiles.astype(jnp.int32), xs, w1, b1p, w2, b2[:, None, :], sel)
    return _combine_call(pos, gates.T, y, ys)


def _t5_bucket(dist):
    n = jnp.maximum(dist, 0)
    max_exact = NUM_BUCKETS // 2
    nf = jnp.maximum(n, max_exact).astype(F32)
    large = max_exact + (jnp.log(nf / max_exact) / math.log(MAX_DISTANCE / max_exact)
                         * (NUM_BUCKETS - max_exact)).astype(jnp.int32)
    large = jnp.minimum(large, NUM_BUCKETS - 1)
    return jnp.where(n < max_exact, n, large)


def _window_bias(dist, rel_bias, n_kv, q_per_kv):
    tq, tk = dist.shape
    bias = jnp.transpose(rel_bias[_t5_bucket(dist)].astype(F32), (2, 0, 1))
    valid = (dist >= 0) & (dist < WINDOW)
    return jnp.where(valid, bias, NEG_INF).reshape(n_kv, q_per_kv * tq, tk)


def _selection_matrix():
    src = jnp.arange(2 * LANES)
    dst = (src % 2) * LANES + src // 2
    return jax.nn.one_hot(dst, 2 * LANES, dtype=BF16)


def kernel(x_prompt, x_sample, cache_k, cache_v, norm_mix_g, norm_ffn_g, a_w_in, a_b_in, a_ln_g,
           a_ln_b, a_w_s, a_b_s, a_w_out, a_b_out, s_w_qkv, s_b_qkv, s_q_norm_g, s_k_norm_g,
           s_sinks, s_w_o, s_b_o, rel_bias, m_w_router, m_b_router, m_w1, m_b1, m_w2, m_b2):
    batch, seq, d = x_prompt.shape
    dec_batch, dec_seq, _ = x_sample.shape
    depth = norm_mix_g.shape[0]
    n_groups = a_w_s.shape[1]
    n_heads = s_sinks.shape[1]
    n_kv, head_dim = cache_k.shape[3], cache_k.shape[4]
    q_per_kv = n_heads // n_kv
    cw = cache_k.shape[2]
    ns = dec_batch * dec_seq
    assert ns % ROW_BLOCK == 0 and seq % ROW_BLOCK == 0 and CHUNK % dec_seq == 0
    assert (ns + batch * seq) % ROUTE_BLOCK == 0 and seq >= WINDOW and SUBLANES % dec_seq == 0

    y = jnp.concatenate([x_sample.reshape(ns, d), x_prompt.reshape(batch * seq, d)], axis=0)
    sel = _selection_matrix()
    ones_bd = jnp.kron(jnp.eye(LANES // head_dim, dtype=F32),
                       jnp.ones((head_dim, head_dim), F32)).astype(BF16)

    qi, kj = jnp.arange(WINDOW), jnp.arange(2 * WINDOW)
    bias_p = _window_bias(WINDOW + qi[:, None] - kj[None, :], rel_bias, n_kv, q_per_kv)
    qt = jnp.arange(dec_seq)
    bias_c = _window_bias(qt[:, None] + cw - jnp.arange(cw)[None, :], rel_bias, n_kv, q_per_kv)
    tn = jnp.arange(SUBLANES)
    dist_n = jnp.where(tn[None, :] < dec_seq, qt[:, None] - tn[None, :], -1)
    bias_n = _window_bias(dist_n, rel_bias, n_kv, q_per_kv)

    row = lambda a: a.reshape(1, -1)
    new_kp, new_vp, new_ks, new_vs, new_va = [], [], [], [], []
    for i in range(depth):
        j = i // 2
        g_mix = row(norm_mix_g[i])
        if i % 2 == 0:
            w_s = jnp.tril(a_w_s[j])
            c = min(dec_seq, CHUNK)
            w_sample = jnp.einsum("ab,gpq->gapbq", jnp.eye(CHUNK // c, dtype=F32),
                                  jnp.tril(a_w_s[j][:, :c, :c])).reshape(n_groups, CHUNK, CHUNK)
            wsp = jnp.stack([w_sample, w_s[:, :CHUNK, :CHUNK]]).astype(BF16)
            dg = a_w_out.shape[1] // n_groups
            b_prompt = jnp.repeat(a_b_s[j][:, :CHUNK].T, dg, axis=1)
            b_sample = jnp.repeat(jnp.tile(a_b_s[j][:, :c].T, (CHUNK // c, 1)), dg, axis=1)
            bsp = jnp.stack([b_sample, b_prompt])
            y, v_s = _sgu_call(y, ns, g_mix, a_w_in[j].astype(BF16), row(a_b_in[j]), row(a_ln_g[j]),
                               row(a_ln_b[j]), wsp, bsp, a_w_out[j].astype(BF16), row(a_b_out[j]))
            new_va.append(v_s.reshape(dec_batch, dec_seq, -1))
        else:
            w_qkv, b_qkv = s_w_qkv[j].astype(BF16), row(s_b_qkv[j])
            qg = row(jnp.tile(s_q_norm_g[j], n_heads))
            kg = row(jnp.tile(s_k_norm_g[j], n_kv))
            w_o, b_o = s_w_o[j].astype(BF16), row(s_b_o[j])
            sinks = s_sinks[j].astype(F32).reshape(n_kv, q_per_kv, 1)
            sink_p = jnp.repeat(sinks, WINDOW, axis=1)
            sink_s = jnp.repeat(sinks, dec_seq, axis=1)
            ck = cache_k[j].reshape(dec_batch, cw, n_kv * head_dim)
            cv = cache_v[j].reshape(dec_batch, cw, n_kv * head_dim)
            y, ks, vs = _swa_sample_call(y, ck, cv, dec_seq, g_mix, w_qkv, b_qkv, qg, kg, ones_bd,
                                         bias_c, bias_n, sink_s, w_o, b_o, n_kv, q_per_kv, head_dim)
            y, kp, vp = _swa_prompt_call(y, ns, batch, seq, g_mix, w_qkv, b_qkv, qg, kg, ones_bd,
                                         bias_p, sink_p, w_o, b_o, n_kv, q_per_kv, head_dim)
            new_ks.append(ks.reshape(dec_batch, cw, n_kv, head_dim))
            new_vs.append(vs.reshape(dec_batch, cw, n_kv, head_dim))
            new_kp.append(kp.reshape(batch, WINDOW, n_kv, head_dim))
            new_vp.append(vp.reshape(batch, WINDOW, n_kv, head_dim))
        y = _moe(y, row(norm_ffn_g[i]), m_w_router[i], m_b_router[i], m_w1[i], m_b1[i], m_w2[i],
                 m_b2[i], sel)
    yp = y[ns:].reshape(batch, seq, d)
    ys = y[:ns].reshape(dec_batch, dec_seq, d)
    return (yp, ys, jnp.stack(new_kp), jnp.stack(new_vp), jnp.stack(new_ks), jnp.stack(new_vs),
            jnp.stack(new_va))
```

```python
import functools
import math

import jax
import jax.numpy as jnp
from jax import lax
from jax.experimental import pallas as pl
from jax.experimental.pallas import tpu as pltpu

EPS = 1e-6
NEG_INF = -1e30
CHUNK = 128
WINDOW = 128
NUM_BUCKETS = 32
MAX_DISTANCE = 128
TOP_K = 4
SWIGLU_LIMIT = 7.0
SWIGLU_ALPHA = 1.702

LANES = 128
SUBLANES = 8
VMEM_LIMIT_BYTES = 56 << 20

ROW_BLOCK = 256
ROUTE_BLOCK = 512
EXPERT_TILE = 256
DISPATCH_BLOCK = 512
COMBINE_BLOCK = 256
SAMPLE_SEQ_BLOCK = 32

F32 = jnp.float32
BF16 = jnp.bfloat16


def _params(*sem):
    return pltpu.CompilerParams(dimension_semantics=sem, vmem_limit_bytes=VMEM_LIMIT_BYTES)


def _rms(x, g):
    return x * lax.rsqrt(jnp.mean(x * x, axis=-1, keepdims=True) + EPS) * g


def _dot(a, b):
    return jnp.dot(a, b, preferred_element_type=F32)


def _dot_nt(a, b):
    return lax.dot_general(a, b, (((1,), (1,)), ((), ())), preferred_element_type=F32)


def _sgu_body(n_sample_blocks, n_groups, y_ref, g_ref, win_ref, bin_ref, lng_ref, lnb_ref,
              wsp_ref, bsp_ref, wout_ref, bout_ref, y2_ref, v_ref):
    i = pl.program_id(0)
    x = y_ref[...]
    xn = _rms(x, g_ref[...]).astype(BF16)
    z = _dot(xn, win_ref[...]) + bin_ref[...]
    z = 0.5 * z * (1.0 + lax.erf(z * (1.0 / math.sqrt(2.0))))
    half = z.shape[1] // 2
    u, v = z[:, :half], z[:, half:]
    mu = jnp.mean(v, axis=-1, keepdims=True)
    vc = v - mu
    vn = vc * lax.rsqrt(jnp.mean(vc * vc, axis=-1, keepdims=True) + EPS) * lng_ref[...] + lnb_ref[...]

    @pl.when(i < n_sample_blocks)
    def _():
        v_ref[...] = vn

    vb = vn.astype(BF16)
    dg = half // n_groups
    rows = []
    for c in range(x.shape[0] // CHUNK):
        r0 = c * CHUNK
        cols = []
        for g in range(n_groups):
            gate = _dot(wsp_ref[0, g], vb[r0:r0 + CHUNK, g * dg:(g + 1) * dg])
            gate = gate + bsp_ref[0, :, g * dg:(g + 1) * dg]
            cols.append(u[r0:r0 + CHUNK, g * dg:(g + 1) * dg] * gate)
        rows.append(jnp.concatenate(cols, axis=1))
    out = jnp.concatenate(rows, axis=0).astype(BF16)
    y2_ref[...] = x + _dot(out, wout_ref[...]) + bout_ref[...]


def _sgu_call(y, n_sample_rows, g, w_in, b_in, ln_g, ln_b, wsp, bsp, w_out, b_out):
    t, d = y.shape
    rb = ROW_BLOCK
    nsb = n_sample_rows // rb
    n_groups = wsp.shape[1]
    half = w_out.shape[0]
    const = lambda shape: pl.BlockSpec(shape, lambda i: (0,) * len(shape))
    which = lambda i: jnp.where(i < nsb, 0, 1)
    return pl.pallas_call(
        functools.partial(_sgu_body, nsb, n_groups),
        grid=(t // rb,),
        in_specs=[
            pl.BlockSpec((rb, d), lambda i: (i, 0)),
            const((1, d)), const(w_in.shape), const((1, 2 * half)), const((1, half)), const((1, half)),
            pl.BlockSpec((1, n_groups, CHUNK, CHUNK), lambda i: (which(i), 0, 0, 0)),
            pl.BlockSpec((1, CHUNK, half), lambda i: (which(i), 0, 0)),
            const(w_out.shape), const((1, d)),
        ],
        out_specs=[
            pl.BlockSpec((rb, d), lambda i: (i, 0)),
            pl.BlockSpec((rb, half), lambda i: (jnp.minimum(i, nsb - 1), 0)),
        ],
        out_shape=[jax.ShapeDtypeStruct((t, d), F32), jax.ShapeDtypeStruct((n_sample_rows, half), F32)],
        compiler_params=_params("arbitrary"),
        name="sgu",
    )(y, g, w_in, b_in, ln_g, ln_b, wsp, bsp, w_out, b_out)


def _head_mean_square(z, ones_bd, head_dim):
    sq = z * z
    tiles = z.shape[1] // LANES
    st = jnp.concatenate([sq[:, c * LANES:(c + 1) * LANES] for c in range(tiles)], axis=0)
    hi = st.astype(BF16)
    lo = (st - hi.astype(F32)).astype(BF16)
    ms = (_dot(hi, ones_bd) + _dot(lo, ones_bd)) * (1.0 / head_dim)
    r = z.shape[0]
    return jnp.concatenate([ms[c * r:(c + 1) * r] for c in range(tiles)], axis=1)


def _qkv(x, g_ref, wqkv_ref, bqkv_ref, qg_ref, kg_ref, ones_ref, n_q, n_k, head_dim):
    xn = _rms(x, g_ref[...]).astype(BF16)
    qkv = _dot(xn, wqkv_ref[...]) + bqkv_ref[...]
    q, k, v = qkv[:, :n_q], qkv[:, n_q:n_q + n_k], qkv[:, n_q + n_k:]
    ones_bd = ones_ref[...]
    q = q * lax.rsqrt(_head_mean_square(q, ones_bd, head_dim) + EPS) * qg_ref[...]
    k = k * lax.rsqrt(_head_mean_square(k, ones_bd, head_dim) + EPS) * kg_ref[...]
    return q, k, v


def _swa_prompt_body(n_kv, q_per_kv, head_dim, y_ref, g_ref, wqkv_ref, bqkv_ref, qg_ref, kg_ref,
                     ones_ref, bias_ref, sink_ref, wo_ref, bo_ref, y2_ref, ko_ref, vo_ref,
                     kprev, vprev):
    n = pl.program_id(1)
    n_q, n_k = n_kv * q_per_kv * head_dim, n_kv * head_dim
    x = y_ref[...]
    q, k, v = _qkv(x, g_ref, wqkv_ref, bqkv_ref, qg_ref, kg_ref, ones_ref, n_q, n_k, head_dim)
    qb = (q * (head_dim ** -0.5)).astype(BF16)
    kb, vb = k.astype(BF16), v.astype(BF16)

    @pl.when(n == 0)
    def _():
        kprev[...] = jnp.zeros_like(kprev)
        vprev[...] = jnp.zeros_like(vprev)

    w = WINDOW
    col = lax.broadcasted_iota(jnp.int32, (q_per_kv * w, 2 * w), 1)
    no_prev = jnp.logical_and(n == 0, col < w)
    att_rows = []
    for c in range(x.shape[0] // w):
        r0 = c * w
        k_prev = kprev[...] if c == 0 else kb[r0 - w:r0]
        v_prev = vprev[...] if c == 0 else vb[r0 - w:r0]
        k2 = jnp.concatenate([k_prev, kb[r0:r0 + w]], axis=0)
        v2 = jnp.concatenate([v_prev, vb[r0:r0 + w]], axis=0)
        heads = []
        for j in range(n_kv):
            qj = jnp.concatenate(
                [qb[r0:r0 + w, (j * q_per_kv + g) * head_dim:(j * q_per_kv + g + 1) * head_dim]
                 for g in range(q_per_kv)], axis=0)
            bias = bias_ref[j]
            if c == 0:
                bias = jnp.where(no_prev, NEG_INF, bias)
            s = _dot_nt(qj, k2[:, j * head_dim:(j + 1) * head_dim]) + bias
            sink = sink_ref[j]
            m = jnp.maximum(jnp.max(s, axis=1, keepdims=True), sink)
            p = jnp.exp(s - m)
            den = jnp.sum(p, axis=1, keepdims=True) + jnp.exp(sink - m)
            o = _dot(p.astype(BF16), v2[:, j * head_dim:(j + 1) * head_dim]) / den
            heads.extend(o[g * w:(g + 1) * w] for g in range(q_per_kv))
        att_rows.append(jnp.concatenate(heads, axis=1))
    att = jnp.concatenate(att_rows, axis=0).astype(BF16)
    y2_ref[...] = x + _dot(att, wo_ref[...]) + bo_ref[...]
    last = x.shape[0] - w
    kprev[...] = kb[last:]
    vprev[...] = vb[last:]

    @pl.when(n == pl.num_programs(1) - 1)
    def _():
        ko_ref[0] = k[last:]
        vo_ref[0] = v[last:]


def _swa_prompt_call(y, n_sample_rows, batch, seq, g, w_qkv, b_qkv, qg, kg, ones_bd, bias, sink,
                     w_o, b_o, n_kv, q_per_kv, head_dim):
    t, d = y.shape
    rb = ROW_BLOCK
    nb = seq // rb
    base = n_sample_rows // rb
    n_k = n_kv * head_dim
    const = lambda shape: pl.BlockSpec(shape, lambda b, n: (0,) * len(shape))
    row_spec = pl.BlockSpec((rb, d), lambda b, n: (base + b * nb + n, 0))
    kv_spec = pl.BlockSpec((1, WINDOW, n_k), lambda b, n: (b, 0, 0))
    return pl.pallas_call(
        functools.partial(_swa_prompt_body, n_kv, q_per_kv, head_dim),
        grid=(batch, nb),
        in_specs=[row_spec, const((1, d)), const(w_qkv.shape), const(b_qkv.shape), const(qg.shape),
                  const(kg.shape), const(ones_bd.shape), const(bias.shape), const(sink.shape),
                  const(w_o.shape), const((1, d))],
        out_specs=[row_spec, kv_spec, kv_spec],
        out_shape=[jax.ShapeDtypeStruct((t, d), F32),
                   jax.ShapeDtypeStruct((batch, WINDOW, n_k), F32),
                   jax.ShapeDtypeStruct((batch, WINDOW, n_k), F32)],
        scratch_shapes=[pltpu.VMEM((WINDOW, n_k), BF16), pltpu.VMEM((WINDOW, n_k), BF16)],
        input_output_aliases={0: 0},
        compiler_params=_params("arbitrary", "arbitrary"),
        name="swa_prompt",
    )(y, g, w_qkv, b_qkv, qg, kg, ones_bd, bias, sink, w_o, b_o)


def _swa_sample_body(n_kv, q_per_kv, head_dim, dec_seq, y_ref, ck_ref, cv_ref, g_ref, wqkv_ref,
                     bqkv_ref, qg_ref, kg_ref, ones_ref, biasc_ref, biasn_ref, sink_ref, wo_ref,
                     bo_ref, y2_ref, ko_ref, vo_ref, q_scr, k_scr, v_scr, att_scr):
    n_q, n_k = n_kv * q_per_kv * head_dim, n_kv * head_dim
    x = y_ref[...]
    q, k, v = _qkv(x, g_ref, wqkv_ref, bqkv_ref, qg_ref, kg_ref, ones_ref, n_q, n_k, head_dim)
    q_scr[...] = q * (head_dim ** -0.5)
    k_scr[...] = k
    v_scr[...] = v
    cw = ck_ref.shape[1]
    pad = jnp.zeros((SUBLANES - dec_seq, n_k), F32)

    def one_seq(s, qs, k_new, v_new):
        ck, cv = ck_ref[s], cv_ref[s]
        ko_ref[s, 0:cw - dec_seq, :] = ck[dec_seq:]
        ko_ref[s, cw - dec_seq:cw, :] = k_new
        vo_ref[s, 0:cw - dec_seq, :] = cv[dec_seq:]
        vo_ref[s, cw - dec_seq:cw, :] = v_new
        qs = qs.astype(BF16)
        ckb, cvb = ck.astype(BF16), cv.astype(BF16)
        kn8 = jnp.concatenate([k_new, pad], axis=0).astype(BF16)
        vn8 = jnp.concatenate([v_new, pad], axis=0).astype(BF16)
        outs = []
        for j in range(n_kv):
            lo, hi = j * head_dim, (j + 1) * head_dim
            qj = jnp.concatenate(
                [qs[:, (j * q_per_kv + g) * head_dim:(j * q_per_kv + g + 1) * head_dim]
                 for g in range(q_per_kv)], axis=0)
            s_c = _dot_nt(qj, ckb[:, lo:hi]) + biasc_ref[j]
            s_n = _dot_nt(qj, kn8[:, lo:hi]) + biasn_ref[j]
            sink = sink_ref[j]
            m = jnp.maximum(jnp.maximum(jnp.max(s_c, axis=1, keepdims=True),
                                        jnp.max(s_n, axis=1, keepdims=True)), sink)
            p_c, p_n = jnp.exp(s_c - m), jnp.exp(s_n - m)
            den = (jnp.sum(p_c, axis=1, keepdims=True) + jnp.sum(p_n, axis=1, keepdims=True)
                   + jnp.exp(sink - m))
            o = (_dot(p_c.astype(BF16), cvb[:, lo:hi]) + _dot(p_n.astype(BF16), vn8[:, lo:hi])) / den
            outs.extend(o[g * dec_seq:(g + 1) * dec_seq] for g in range(q_per_kv))
        return jnp.concatenate(outs, axis=1)

    per_group = SUBLANES // dec_seq

    def per_group_of_rows(p, carry):
        r = pl.multiple_of(p * SUBLANES, SUBLANES)
        q8, k8, v8 = q_scr[pl.ds(r, SUBLANES), :], k_scr[pl.ds(r, SUBLANES), :], v_scr[pl.ds(r, SUBLANES), :]
        outs = []
        for u in range(per_group):
            rows = slice(u * dec_seq, (u + 1) * dec_seq)
            outs.append(one_seq(p * per_group + u, q8[rows], k8[rows], v8[rows]))
        att_scr[pl.ds(r, SUBLANES), :] = jnp.concatenate(outs, axis=0)
        return carry

    lax.fori_loop(0, ck_ref.shape[0] // per_group, per_group_of_rows, 0)
    y2_ref[...] = x + _dot(att_scr[...].astype(BF16), wo_ref[...]) + bo_ref[...]


def _swa_sample_call(y, ck, cv, dec_seq, g, w_qkv, b_qkv, qg, kg, ones_bd, bias_c, bias_n, sink,
                     w_o, b_o, n_kv, q_per_kv, head_dim):
    t, d = y.shape
    dec_batch, cw, n_k = ck.shape
    sb = min(SAMPLE_SEQ_BLOCK, dec_batch)
    rb = sb * dec_seq
    n_q = n_kv * q_per_kv * head_dim
    const = lambda shape: pl.BlockSpec(shape, lambda i: (0,) * len(shape))
    row_spec = pl.BlockSpec((rb, d), lambda i: (i, 0))
    cache_spec = pl.BlockSpec((sb, cw, n_k), lambda i: (i, 0, 0))
    return pl.pallas_call(
        functools.partial(_swa_sample_body, n_kv, q_per_kv, head_dim, dec_seq),
        grid=(dec_batch // sb,),
        in_specs=[row_spec, cache_spec, cache_spec, const((1, d)), const(w_qkv.shape),
                  const(b_qkv.shape), const(qg.shape), const(kg.shape), const(ones_bd.shape),
                  const(bias_c.shape), const(bias_n.shape), const(sink.shape), const(w_o.shape),
                  const((1, d))],
        out_specs=[row_spec, cache_spec, cache_spec],
        out_shape=[jax.ShapeDtypeStruct((t, d), F32),
                   jax.ShapeDtypeStruct(ck.shape, F32), jax.ShapeDtypeStruct(cv.shape, F32)],
        scratch_shapes=[pltpu.VMEM((rb, n_q), F32), pltpu.VMEM((rb, n_k), F32),
                        pltpu.VMEM((rb, n_k), F32), pltpu.VMEM((rb, n_q), F32)],
        input_output_aliases={0: 0},
        compiler_params=_params("arbitrary"),
        name="swa_sample",
    )(y, ck, cv, g, w_qkv, b_qkv, qg, kg, ones_bd, bias_c, bias_n, sink, w_o, b_o)


def _route_body(y_ref, g_ref, wr_ref, br_ref, xn_ref, ids_ref, gates_ref, rank_ref, tot_ref, carry):
    i = pl.program_id(0)

    @pl.when(i == 0)
    def _():
        carry[...] = jnp.zeros_like(carry)

    xn = _rms(y_ref[...], g_ref[...])
    xn_ref[...] = xn
    logits = lax.dot_general(wr_ref[...], xn, (((1,), (1,)), ((), ())),
                             precision=lax.Precision.HIGHEST, preferred_element_type=F32)
    logits = logits + br_ref[:, 0:1]
    n_exp, tb = logits.shape
    e_iota = lax.broadcasted_iota(jnp.int32, (n_exp, tb), 0)
    work = logits
    vals, ids, hots = [], [], []
    for _ in range(TOP_K):
        m = jnp.max(work, axis=0, keepdims=True)
        idx = jnp.min(jnp.where(work == m, e_iota, n_exp), axis=0, keepdims=True)
        hot = e_iota == idx
        vals.append(m)
        ids.append(idx)
        hots.append(hot)
        work = jnp.where(hot, -jnp.inf, work)
    ex = [jnp.exp(v - vals[0]) for v in vals]
    den = ex[0] + ex[1] + ex[2] + ex[3]
    ids_ref[...] = jnp.concatenate(ids, axis=0)
    gates_ref[...] = jnp.concatenate([e / den for e in ex], axis=0)

    hot_all = jnp.zeros((n_exp, tb), F32)
    for hot in hots:
        hot_all = hot_all + hot.astype(F32)
    before = (lax.broadcasted_iota(jnp.int32, (tb, tb), 0)
              < lax.broadcasted_iota(jnp.int32, (tb, tb), 1)).astype(BF16)
    count = _dot(hot_all.astype(BF16), before) + carry[:, 0:1]
    rank_ref[...] = jnp.concatenate(
        [jnp.sum(jnp.where(hot, count, 0.0), axis=0, keepdims=True) for hot in hots],
        axis=0).astype(jnp.int32)
    total = carry[...] + jnp.sum(hot_all, axis=1, keepdims=True)
    carry[...] = total
    tot_ref[...] = total.astype(jnp.int32)


def _route_call(y, g, w_r_t, b_r):
    t, d = y.shape
    n_exp = w_r_t.shape[0]
    tb = ROUTE_BLOCK
    const = lambda shape: pl.BlockSpec(shape, lambda i: (0,) * len(shape))
    tok = lambda dtype: jax.ShapeDtypeStruct((TOP_K, t), dtype)
    tok_spec = pl.BlockSpec((TOP_K, tb), lambda i: (0, i))
    return pl.pallas_call(
        _route_body,
        grid=(t // tb,),
        in_specs=[pl.BlockSpec((tb, d), lambda i: (i, 0)), const((1, d)), const((n_exp, d)),
                  const((n_exp, LANES))],
        out_specs=[pl.BlockSpec((tb, d), lambda i: (i, 0)), tok_spec, tok_spec, tok_spec,
                   const((n_exp, LANES))],
        out_shape=[jax.ShapeDtypeStruct((t, d), F32), tok(jnp.int32), tok(F32), tok(jnp.int32),
                   jax.ShapeDtypeStruct((n_exp, LANES), jnp.int32)],
        scratch_shapes=[pltpu.VMEM((n_exp, LANES), F32)],
        compiler_params=_params("arbitrary"),
        name="moe_route",
    )(y, g, w_r_t, b_r)


def _row_copy(src, src_row, dst, dst_row, sem):
    return pltpu.make_async_copy(src.at[pl.ds(src_row, 1)], dst.at[pl.ds(dst_row, 1)], sem)


def _dispatch_body(tb, tile, ztile_ref, pos_ref, xn_ref, xs_hbm, zeros, sem, zsem):
    i = pl.program_id(0)

    @pl.when(i == 0)
    def _():
        zeros[...] = jnp.zeros_like(zeros)

        def fill(e):
            start = pl.multiple_of(ztile_ref[e], tile)
            return pltpu.make_async_copy(zeros, xs_hbm.at[pl.ds(start, tile)], zsem)

        for e in range(ztile_ref.shape[0]):
            @pl.when(ztile_ref[e] >= 0)
            def _():
                fill(e).start()
        for e in range(ztile_ref.shape[0]):
            @pl.when(ztile_ref[e] >= 0)
            def _():
                fill(e).wait()

    def issue(t, carry):
        for k in range(TOP_K):
            _row_copy(xn_ref, t, xs_hbm, pos_ref[k, t], sem).start()
        return carry

    lax.fori_loop(0, tb, issue, 0, unroll=8)
    for k in range(TOP_K):
        pltpu.make_async_copy(xn_ref, xs_hbm.at[pl.ds(0, tb)], sem).wait()


def _dispatch_call(pos, ztile, xn, n_rows_padded):
    t, d = xn.shape
    tb = DISPATCH_BLOCK
    tile = EXPERT_TILE
    return pl.pallas_call(
        functools.partial(_dispatch_body, tb, tile),
        grid_spec=pltpu.PrefetchScalarGridSpec(
            num_scalar_prefetch=1,
            grid=(t // tb,),
            in_specs=[pl.BlockSpec((TOP_K, tb), lambda i, z: (0, i), memory_space=pltpu.SMEM),
                      pl.BlockSpec((tb, d), lambda i, z: (i, 0))],
            out_specs=pl.BlockSpec(memory_space=pl.ANY),
            scratch_shapes=[pltpu.VMEM((tile, d), F32), pltpu.SemaphoreType.DMA,
                            pltpu.SemaphoreType.DMA]),
        out_shape=jax.ShapeDtypeStruct((n_rows_padded, d), F32),
        compiler_params=_params("arbitrary"),
        name="moe_dispatch",
    )(ztile, pos, xn)


def _expert_body(te_ref, nt_ref, xs_ref, w1_ref, b1_ref, w2_ref, b2_ref, sel_ref, ys_ref, w1p, w2b):
    i = pl.program_id(0)
    prev = te_ref[jnp.maximum(i - 1, 0)]
    fresh = jnp.logical_or(i == 0, te_ref[i] != prev)
    live = i < nt_ref[0]
    d_ff2 = w1p.shape[1]
    pair = 2 * LANES

    @pl.when(jnp.logical_and(live, fresh))
    def _():
        sel = sel_ref[...]
        for c in range(d_ff2 // pair):
            blk = w1_ref[0, 0, :, c * pair:(c + 1) * pair].astype(BF16)
            w1p[:, c * pair:(c + 1) * pair] = _dot(blk, sel).astype(BF16)
        w2b[...] = w2_ref[0, 0].astype(BF16)

    @pl.when(live)
    def _():
        x = xs_ref[...].astype(BF16)
        h = _dot(x, w1p[...]) + b1_ref[0, 0]
        acts = []
        for c in range(d_ff2 // pair):
            gate = jnp.minimum(h[:, c * pair:c * pair + LANES], SWIGLU_LIMIT)
            up = jnp.clip(h[:, c * pair + LANES:(c + 1) * pair], -SWIGLU_LIMIT, SWIGLU_LIMIT)
            acts.append((up + 1.0) * (gate * (1.0 / (1.0 + jnp.exp(-SWIGLU_ALPHA * gate)))))
        a = jnp.concatenate(acts, axis=1).astype(BF16)
        ys_ref[...] = _dot(a, w2b[...]) + b2_ref[0, 0]

    @pl.when(jnp.logical_not(live))
    def _():
        ys_ref[...] = jnp.zeros_like(ys_ref)


def _expert_call(layer, tile_expert, n_tiles, xs, w1, b1p, w2, b2, sel):
    r, d = xs.shape
    tile = EXPERT_TILE
    d_ff2 = w1.shape[3]
    d_ff = w2.shape[2]
    row = lambda i, te, nt: (i, 0)
    exp4 = lambda i, te, nt: (layer, te[i], 0, 0)
    return pl.pallas_call(
        _expert_body,
        grid_spec=pltpu.PrefetchScalarGridSpec(
            num_scalar_prefetch=2,
            grid=(r // tile,),
            in_specs=[pl.BlockSpec((tile, d), row),
                      pl.BlockSpec((1, 1, d, d_ff2), exp4), pl.BlockSpec((1, 1, 1, d_ff2), exp4),
                      pl.BlockSpec((1, 1, d_ff, d), exp4), pl.BlockSpec((1, 1, 1, d), exp4),
                      pl.BlockSpec(sel.shape, lambda i, te, nt: (0, 0))],
            out_specs=pl.BlockSpec((tile, d), row),
            scratch_shapes=[pltpu.VMEM((d, d_ff2), BF16), pltpu.VMEM((d_ff, d), BF16)]),
        out_shape=jax.ShapeDtypeStruct((r, d), F32),
        compiler_params=_params("arbitrary"),
        name="moe_expert",
    )(tile_expert, n_tiles, xs, w1, b1p, w2, b2, sel)


def _combine_body(tb, pos_ref, posn_ref, gates_ref, y_ref, ys_hbm, out_ref, buf, sem):
    i = pl.program_id(0)
    nsteps = pl.num_programs(0)
    slot = i % 2

    def issue(table, dst_slot):
        def one(t, carry):
            for k in range(TOP_K):
                pltpu.make_async_copy(ys_hbm.at[pl.ds(table[k, t], 1)],
                                      buf.at[dst_slot, k, pl.ds(t, 1)], sem.at[dst_slot]).start()
            return carry
        lax.fori_loop(0, tb, one, 0, unroll=8)

    @pl.when(i == 0)
    def _():
        issue(pos_ref, 0)

    @pl.when(i + 1 < nsteps)
    def _():
        issue(posn_ref, 1 - slot)

    for k in range(TOP_K):
        pltpu.make_async_copy(ys_hbm.at[pl.ds(0, tb)], buf.at[slot, k], sem.at[slot]).wait()
    acc = y_ref[...]
    for k in range(TOP_K):
        acc = acc + gates_ref[:, k:k + 1] * buf[slot, k]
    out_ref[...] = acc


def _combine_call(pos, gates_t, y, ys):
    t, d = y.shape
    tb = COMBINE_BLOCK
    nb = t // tb
    smem = lambda fn: pl.BlockSpec((TOP_K, tb), fn, memory_space=pltpu.SMEM)
    return pl.pallas_call(
        functools.partial(_combine_body, tb),
        grid=(nb,),
        in_specs=[smem(lambda i: (0, i)), smem(lambda i: (0, jnp.minimum(i + 1, nb - 1))),
                  pl.BlockSpec((tb, TOP_K), lambda i: (i, 0)),
                  pl.BlockSpec((tb, d), lambda i: (i, 0)),
                  pl.BlockSpec(memory_space=pl.ANY)],
        out_specs=pl.BlockSpec((tb, d), lambda i: (i, 0)),
        out_shape=jax.ShapeDtypeStruct((t, d), F32),
        scratch_shapes=[pltpu.VMEM((2, TOP_K, tb, d), F32), pltpu.SemaphoreType.DMA((2,))],
        compiler_params=_params("arbitrary"),
        name="moe_combine",
    )(pos, pos, gates_t, y, ys)


def _moe(layer, y, g, w_r, b_r, w1, b1p, w2, b2, sel):
    t, d = y.shape
    n_exp = w_r.shape[1]
    tile = EXPERT_TILE
    max_tiles = (TOP_K * t) // tile + n_exp
    b_r_l = jnp.broadcast_to(b_r[:, None], (n_exp, LANES))
    xn, ids, gates, rank, totals = _route_call(y, g, w_r.T, b_r_l)

    n_e = totals[:, 0]
    tiles_e = (n_e + tile - 1) // tile
    tile_end = jnp.cumsum(tiles_e)
    row_start = (tile_end - tiles_e) * tile
    n_tiles = tile_end[-1:]
    pos = rank
    for e in range(n_exp):
        pos = pos + jnp.where(ids == e, row_start[e], 0)
    tile_ids = jnp.minimum(jnp.arange(max_tiles, dtype=jnp.int32), n_tiles[0] - 1)
    tile_expert = jnp.sum(tile_end[None, :] <= tile_ids[:, None], axis=1).astype(jnp.int32)
    spare = n_tiles[0] + jnp.arange(n_exp, dtype=jnp.int32)
    ztile = jnp.concatenate([jnp.where(n_e % tile != 0, (tile_end - 1) * tile, -1),
                             jnp.where(spare < max_tiles, spare * tile, -1)]).astype(jnp.int32)

    xs = _dispatch_call(pos, ztile, xn, max_tiles * tile)
    ys = _expert_call(layer, tile_expert, n_tiles.astype(jnp.int32), xs, w1, b1p, w2, b2, sel)
    return _combine_call(pos, gates.T, y, ys)


def _t5_bucket(dist):
    n = jnp.maximum(dist, 0)
    max_exact = NUM_BUCKETS // 2
    nf = jnp.maximum(n, max_exact).astype(F32)
    large = max_exact + (jnp.log(nf / max_exact) / math.log(MAX_DISTANCE / max_exact)
                         * (NUM_BUCKETS - max_exact)).astype(jnp.int32)
    large = jnp.minimum(large, NUM_BUCKETS - 1)
    return jnp.where(n < max_exact, n, large)


def _window_bias(dist, rel_bias, n_kv, q_per_kv):
    tq, tk = dist.shape
    hot = jax.nn.one_hot(_t5_bucket(dist), NUM_BUCKETS, dtype=F32)
    bias = jnp.einsum("qkb,bh->hqk", hot, rel_bias.astype(F32), precision=lax.Precision.HIGHEST)
    valid = (dist >= 0) & (dist < WINDOW)
    return jnp.where(valid, bias, NEG_INF).reshape(n_kv, q_per_kv * tq, tk)


def _selection_matrix():
    src = jnp.arange(2 * LANES)
    dst = (src % 2) * LANES + src // 2
    return jax.nn.one_hot(dst, 2 * LANES, dtype=BF16)


def kernel(x_prompt, x_sample, cache_k, cache_v, norm_mix_g, norm_ffn_g, a_w_in, a_b_in, a_ln_g,
           a_ln_b, a_w_s, a_b_s, a_w_out, a_b_out, s_w_qkv, s_b_qkv, s_q_norm_g, s_k_norm_g,
           s_sinks, s_w_o, s_b_o, rel_bias, m_w_router, m_b_router, m_w1, m_b1, m_w2, m_b2):
    batch, seq, d = x_prompt.shape
    dec_batch, dec_seq, _ = x_sample.shape
    depth = norm_mix_g.shape[0]
    n_groups = a_w_s.shape[1]
    n_heads = s_sinks.shape[1]
    n_kv, head_dim = cache_k.shape[3], cache_k.shape[4]
    q_per_kv = n_heads // n_kv
    cw = cache_k.shape[2]
    ns = dec_batch * dec_seq
    assert ns % ROW_BLOCK == 0 and seq % ROW_BLOCK == 0 and CHUNK % dec_seq == 0
    assert (ns + batch * seq) % ROUTE_BLOCK == 0 and seq >= WINDOW and SUBLANES % dec_seq == 0

    y = jnp.concatenate([x_sample.reshape(ns, d), x_prompt.reshape(batch * seq, d)], axis=0)
    sel = _selection_matrix()
    ones_bd = jnp.kron(jnp.eye(LANES // head_dim, dtype=F32),
                       jnp.ones((head_dim, head_dim), F32)).astype(BF16)

    qi, kj = jnp.arange(WINDOW), jnp.arange(2 * WINDOW)
    bias_p = _window_bias(WINDOW + qi[:, None] - kj[None, :], rel_bias, n_kv, q_per_kv)
    qt = jnp.arange(dec_seq)
    bias_c = _window_bias(qt[:, None] + cw - jnp.arange(cw)[None, :], rel_bias, n_kv, q_per_kv)
    tn = jnp.arange(SUBLANES)
    dist_n = jnp.where(tn[None, :] < dec_seq, qt[:, None] - tn[None, :], -1)
    bias_n = _window_bias(dist_n, rel_bias, n_kv, q_per_kv)

    n_exp, d_ff2 = m_b1.shape[1], m_b1.shape[2]
    b1p = (m_b1.reshape(depth, n_exp, d_ff2 // (2 * LANES), LANES, 2).swapaxes(3, 4)
           .reshape(depth, n_exp, 1, d_ff2))
    b2 = m_b2[:, :, None, :]

    row = lambda a: a.reshape(1, -1)
    new_kp, new_vp, new_ks, new_vs, new_va = [], [], [], [], []
    for i in range(depth):
        j = i // 2
        g_mix = row(norm_mix_g[i])
        if i % 2 == 0:
            w_s = jnp.tril(a_w_s[j])
            c = min(dec_seq, CHUNK)
            w_sample = jnp.einsum("ab,gpq->gapbq", jnp.eye(CHUNK // c, dtype=F32),
                                  jnp.tril(a_w_s[j][:, :c, :c])).reshape(n_groups, CHUNK, CHUNK)
            wsp = jnp.stack([w_sample, w_s[:, :CHUNK, :CHUNK]]).astype(BF16)
            dg = a_w_out.shape[1] // n_groups
            b_prompt = jnp.repeat(a_b_s[j][:, :CHUNK].T, dg, axis=1)
            b_sample = jnp.repeat(jnp.tile(a_b_s[j][:, :c].T, (CHUNK // c, 1)), dg, axis=1)
            bsp = jnp.stack([b_sample, b_prompt])
            y, v_s = _sgu_call(y, ns, g_mix, a_w_in[j].astype(BF16), row(a_b_in[j]), row(a_ln_g[j]),
                               row(a_ln_b[j]), wsp, bsp, a_w_out[j].astype(BF16), row(a_b_out[j]))
            new_va.append(v_s.reshape(dec_batch, dec_seq, -1))
        else:
            w_qkv, b_qkv = s_w_qkv[j].astype(BF16), row(s_b_qkv[j])
            qg = row(jnp.tile(s_q_norm_g[j], n_heads))
            kg = row(jnp.tile(s_k_norm_g[j], n_kv))
            w_o, b_o = s_w_o[j].astype(BF16), row(s_b_o[j])
            sinks = s_sinks[j].astype(F32).reshape(n_kv, q_per_kv, 1)
            sink_p = jnp.repeat(sinks, WINDOW, axis=1)
            sink_s = jnp.repeat(sinks, dec_seq, axis=1)
            ck = cache_k[j].reshape(dec_batch, cw, n_kv * head_dim)
            cv = cache_v[j].reshape(dec_batch, cw, n_kv * head_dim)
            y, ks, vs = _swa_sample_call(y, ck, cv, dec_seq, g_mix, w_qkv, b_qkv, qg, kg, ones_bd,
                                         bias_c, bias_n, sink_s, w_o, b_o, n_kv, q_per_kv, head_dim)
            y, kp, vp = _swa_prompt_call(y, ns, batch, seq, g_mix, w_qkv, b_qkv, qg, kg, ones_bd,
                                         bias_p, sink_p, w_o, b_o, n_kv, q_per_kv, head_dim)
            new_ks.append(ks.reshape(dec_batch, cw, n_kv, head_dim))
            new_vs.append(vs.reshape(dec_batch, cw, n_kv, head_dim))
            new_kp.append(kp.reshape(batch, WINDOW, n_kv, head_dim))
            new_vp.append(vp.reshape(batch, WINDOW, n_kv, head_dim))
        y = _moe(i, y, row(norm_ffn_g[i]), m_w_router[i], m_b_router[i], m_w1, b1p, m_w2, b2, sel)
    yp = y[ns:].reshape(batch, seq, d)
    ys = y[:ns].reshape(dec_batch, dec_seq, d)
    return (yp, ys, jnp.stack(new_kp), jnp.stack(new_vp), jnp.stack(new_ks), jnp.stack(new_vs),
            jnp.stack(new_va))
```

```python
import functools
import math

import jax
import jax.numpy as jnp
from jax import lax
from jax.experimental import pallas as pl
from jax.experimental.pallas import tpu as pltpu

EPS = 1e-6
NEG_INF = -1e30
CHUNK = 128
WINDOW = 128
NUM_BUCKETS = 32
MAX_DISTANCE = 128
TOP_K = 4
SWIGLU_LIMIT = 7.0
SWIGLU_ALPHA = 1.702

LANES = 128
SUBLANES = 8
VMEM_LIMIT_BYTES = 56 << 20

ROW_BLOCK = 256
ROUTE_BLOCK = 512
EXPERT_TILE = 256
DISPATCH_BLOCK = 512
COMBINE_BLOCK = 256
SAMPLE_SEQ_BLOCK = 32

F32 = jnp.float32
BF16 = jnp.bfloat16


def _params(*sem):
    return pltpu.CompilerParams(dimension_semantics=sem, vmem_limit_bytes=VMEM_LIMIT_BYTES)


def _rms(x, g):
    return x * lax.rsqrt(jnp.mean(x * x, axis=-1, keepdims=True) + EPS) * g


def _dot(a, b):
    return jnp.dot(a, b, preferred_element_type=F32)


def _dot_nt(a, b):
    return lax.dot_general(a, b, (((1,), (1,)), ((), ())), preferred_element_type=F32)


def _sgu_body(n_sample_blocks, n_groups, y_ref, g_ref, win_ref, bin_ref, lng_ref, lnb_ref,
              wsp_ref, bsp_ref, wout_ref, bout_ref, y2_ref, v_ref):
    i = pl.program_id(0)
    x = y_ref[...]
    xn = _rms(x, g_ref[...]).astype(BF16)
    z = _dot(xn, win_ref[...]) + bin_ref[...]
    z = 0.5 * z * (1.0 + lax.erf(z * (1.0 / math.sqrt(2.0))))
    half = z.shape[1] // 2
    u, v = z[:, :half], z[:, half:]
    mu = jnp.mean(v, axis=-1, keepdims=True)
    vc = v - mu
    vn = vc * lax.rsqrt(jnp.mean(vc * vc, axis=-1, keepdims=True) + EPS) * lng_ref[...] + lnb_ref[...]

    @pl.when(i < n_sample_blocks)
    def _():
        v_ref[...] = vn

    vb = vn.astype(BF16)
    dg = half // n_groups
    rows = []
    for c in range(x.shape[0] // CHUNK):
        r0 = c * CHUNK
        cols = []
        for g in range(n_groups):
            gate = _dot(wsp_ref[0, g], vb[r0:r0 + CHUNK, g * dg:(g + 1) * dg])
            gate = gate + bsp_ref[0, :, g * dg:(g + 1) * dg]
            cols.append(u[r0:r0 + CHUNK, g * dg:(g + 1) * dg] * gate)
        rows.append(jnp.concatenate(cols, axis=1))
    out = jnp.concatenate(rows, axis=0).astype(BF16)
    y2_ref[...] = x + _dot(out, wout_ref[...]) + bout_ref[...]


def _sgu_call(y, n_sample_rows, g, w_in, b_in, ln_g, ln_b, wsp, bsp, w_out, b_out):
    t, d = y.shape
    rb = ROW_BLOCK
    nsb = n_sample_rows // rb
    n_groups = wsp.shape[1]
    half = w_out.shape[0]
    const = lambda shape: pl.BlockSpec(shape, lambda i: (0,) * len(shape))
    which = lambda i: jnp.where(i < nsb, 0, 1)
    return pl.pallas_call(
        functools.partial(_sgu_body, nsb, n_groups),
        grid=(t // rb,),
        in_specs=[
            pl.BlockSpec((rb, d), lambda i: (i, 0)),
            const((1, d)), const(w_in.shape), const((1, 2 * half)), const((1, half)), const((1, half)),
            pl.BlockSpec((1, n_groups, CHUNK, CHUNK), lambda i: (which(i), 0, 0, 0)),
            pl.BlockSpec((1, CHUNK, half), lambda i: (which(i), 0, 0)),
            const(w_out.shape), const((1, d)),
        ],
        out_specs=[
            pl.BlockSpec((rb, d), lambda i: (i, 0)),
            pl.BlockSpec((rb, half), lambda i: (jnp.minimum(i, nsb - 1), 0)),
        ],
        out_shape=[jax.ShapeDtypeStruct((t, d), F32), jax.ShapeDtypeStruct((n_sample_rows, half), F32)],
        compiler_params=_params("arbitrary"),
        name="sgu",
    )(y, g, w_in, b_in, ln_g, ln_b, wsp, bsp, w_out, b_out)


def _head_mean_square(z, ones_bd, head_dim):
    sq = z * z
    tiles = z.shape[1] // LANES
    st = jnp.concatenate([sq[:, c * LANES:(c + 1) * LANES] for c in range(tiles)], axis=0)
    hi = st.astype(BF16)
    lo = (st - hi.astype(F32)).astype(BF16)
    ms = (_dot(hi, ones_bd) + _dot(lo, ones_bd)) * (1.0 / head_dim)
    r = z.shape[0]
    return jnp.concatenate([ms[c * r:(c + 1) * r] for c in range(tiles)], axis=1)


def _qkv(x, g_ref, wqkv_ref, bqkv_ref, qg_ref, kg_ref, ones_ref, n_q, n_k, head_dim):
    xn = _rms(x, g_ref[...]).astype(BF16)
    qkv = _dot(xn, wqkv_ref[...]) + bqkv_ref[...]
    q, k, v = qkv[:, :n_q], qkv[:, n_q:n_q + n_k], qkv[:, n_q + n_k:]
    ones_bd = ones_ref[...]
    q = q * lax.rsqrt(_head_mean_square(q, ones_bd, head_dim) + EPS) * qg_ref[...]
    k = k * lax.rsqrt(_head_mean_square(k, ones_bd, head_dim) + EPS) * kg_ref[...]
    return q, k, v


def _swa_prompt_body(n_kv, q_per_kv, head_dim, y_ref, g_ref, wqkv_ref, bqkv_ref, qg_ref, kg_ref,
                     ones_ref, bias_ref, sink_ref, wo_ref, bo_ref, y2_ref, ko_ref, vo_ref,
                     kprev, vprev):
    n = pl.program_id(1)
    n_q, n_k = n_kv * q_per_kv * head_dim, n_kv * head_dim
    x = y_ref[...]
    q, k, v = _qkv(x, g_ref, wqkv_ref, bqkv_ref, qg_ref, kg_ref, ones_ref, n_q, n_k, head_dim)
    qb = (q * (head_dim ** -0.5)).astype(BF16)
    kb, vb = k.astype(BF16), v.astype(BF16)

    @pl.when(n == 0)
    def _():
        kprev[...] = jnp.zeros_like(kprev)
        vprev[...] = jnp.zeros_like(vprev)

    w = WINDOW
    col = lax.broadcasted_iota(jnp.int32, (q_per_kv * w, 2 * w), 1)
    no_prev = jnp.logical_and(n == 0, col < w)
    att_rows = []
    for c in range(x.shape[0] // w):
        r0 = c * w
        k_prev = kprev[...] if c == 0 else kb[r0 - w:r0]
        v_prev = vprev[...] if c == 0 else vb[r0 - w:r0]
        k2 = jnp.concatenate([k_prev, kb[r0:r0 + w]], axis=0)
        v2 = jnp.concatenate([v_prev, vb[r0:r0 + w]], axis=0)
        heads = []
        for j in range(n_kv):
            qj = jnp.concatenate(
                [qb[r0:r0 + w, (j * q_per_kv + g) * head_dim:(j * q_per_kv + g + 1) * head_dim]
                 for g in range(q_per_kv)], axis=0)
            bias = bias_ref[j]
            if c == 0:
                bias = jnp.where(no_prev, NEG_INF, bias)
            s = _dot_nt(qj, k2[:, j * head_dim:(j + 1) * head_dim]) + bias
            sink = sink_ref[j]
            m = jnp.maximum(jnp.max(s, axis=1, keepdims=True), sink)
            p = jnp.exp(s - m)
            den = jnp.sum(p, axis=1, keepdims=True) + jnp.exp(sink - m)
            o = _dot(p.astype(BF16), v2[:, j * head_dim:(j + 1) * head_dim]) / den
            heads.extend(o[g * w:(g + 1) * w] for g in range(q_per_kv))
        att_rows.append(jnp.concatenate(heads, axis=1))
    att = jnp.concatenate(att_rows, axis=0).astype(BF16)
    y2_ref[...] = x + _dot(att, wo_ref[...]) + bo_ref[...]
    last = x.shape[0] - w
    kprev[...] = kb[last:]
    vprev[...] = vb[last:]

    @pl.when(n == pl.num_programs(1) - 1)
    def _():
        ko_ref[0] = k[last:]
        vo_ref[0] = v[last:]


def _swa_prompt_call(y, n_sample_rows, batch, seq, g, w_qkv, b_qkv, qg, kg, ones_bd, bias, sink,
                     w_o, b_o, n_kv, q_per_kv, head_dim):
    t, d = y.shape
    rb = ROW_BLOCK
    nb = seq // rb
    base = n_sample_rows // rb
    n_k = n_kv * head_dim
    const = lambda shape: pl.BlockSpec(shape, lambda b, n: (0,) * len(shape))
    row_spec = pl.BlockSpec((rb, d), lambda b, n: (base + b * nb + n, 0))
    kv_spec = pl.BlockSpec((1, WINDOW, n_k), lambda b, n: (b, 0, 0))
    return pl.pallas_call(
        functools.partial(_swa_prompt_body, n_kv, q_per_kv, head_dim),
        grid=(batch, nb),
        in_specs=[row_spec, const((1, d)), const(w_qkv.shape), const(b_qkv.shape), const(qg.shape),
                  const(kg.shape), const(ones_bd.shape), const(bias.shape), const(sink.shape),
                  const(w_o.shape), const((1, d))],
        out_specs=[row_spec, kv_spec, kv_spec],
        out_shape=[jax.ShapeDtypeStruct((t, d), F32),
                   jax.ShapeDtypeStruct((batch, WINDOW, n_k), F32),
                   jax.ShapeDtypeStruct((batch, WINDOW, n_k), F32)],
        scratch_shapes=[pltpu.VMEM((WINDOW, n_k), BF16), pltpu.VMEM((WINDOW, n_k), BF16)],
        input_output_aliases={0: 0},
        compiler_params=_params("arbitrary", "arbitrary"),
        name="swa_prompt",
    )(y, g, w_qkv, b_qkv, qg, kg, ones_bd, bias, sink, w_o, b_o)


def _swa_sample_body(n_kv, q_per_kv, head_dim, dec_seq, y_ref, ck_ref, cv_ref, g_ref, wqkv_ref,
                     bqkv_ref, qg_ref, kg_ref, ones_ref, biasc_ref, biasn_ref, sink_ref, wo_ref,
                     bo_ref, y2_ref, ko_ref, vo_ref, q_scr, k_scr, v_scr, att_scr):
    n_q, n_k = n_kv * q_per_kv * head_dim, n_kv * head_dim
    x = y_ref[...]
    q, k, v = _qkv(x, g_ref, wqkv_ref, bqkv_ref, qg_ref, kg_ref, ones_ref, n_q, n_k, head_dim)
    q_scr[...] = q * (head_dim ** -0.5)
    k_scr[...] = k
    v_scr[...] = v
    cw = ck_ref.shape[1]
    pad = jnp.zeros((SUBLANES - dec_seq, n_k), F32)

    def one_seq(s, qs, k_new, v_new):
        ck, cv = ck_ref[s], cv_ref[s]
        ko_ref[s, 0:cw - dec_seq, :] = ck[dec_seq:]
        ko_ref[s, cw - dec_seq:cw, :] = k_new
        vo_ref[s, 0:cw - dec_seq, :] = cv[dec_seq:]
        vo_ref[s, cw - dec_seq:cw, :] = v_new
        qs = qs.astype(BF16)
        ckb, cvb = ck.astype(BF16), cv.astype(BF16)
        kn8 = jnp.concatenate([k_new, pad], axis=0).astype(BF16)
        vn8 = jnp.concatenate([v_new, pad], axis=0).astype(BF16)
        outs = []
        for j in range(n_kv):
            lo, hi = j * head_dim, (j + 1) * head_dim
            qj = jnp.concatenate(
                [qs[:, (j * q_per_kv + g) * head_dim:(j * q_per_kv + g + 1) * head_dim]
                 for g in range(q_per_kv)], axis=0)
            s_c = _dot_nt(qj, ckb[:, lo:hi]) + biasc_ref[j]
            s_n = _dot_nt(qj, kn8[:, lo:hi]) + biasn_ref[j]
            sink = sink_ref[j]
            m = jnp.maximum(jnp.maximum(jnp.max(s_c, axis=1, keepdims=True),
                                        jnp.max(s_n, axis=1, keepdims=True)), sink)
            p_c, p_n = jnp.exp(s_c - m), jnp.exp(s_n - m)
            den = (jnp.sum(p_c, axis=1, keepdims=True) + jnp.sum(p_n, axis=1, keepdims=True)
                   + jnp.exp(sink - m))
            o = (_dot(p_c.astype(BF16), cvb[:, lo:hi]) + _dot(p_n.astype(BF16), vn8[:, lo:hi])) / den
            outs.extend(o[g * dec_seq:(g + 1) * dec_seq] for g in range(q_per_kv))
        return jnp.concatenate(outs, axis=1)

    per_group = SUBLANES // dec_seq

    def per_group_of_rows(p, carry):
        r = pl.multiple_of(p * SUBLANES, SUBLANES)
        q8, k8, v8 = q_scr[pl.ds(r, SUBLANES), :], k_scr[pl.ds(r, SUBLANES), :], v_scr[pl.ds(r, SUBLANES), :]
        outs = []
        for u in range(per_group):
            rows = slice(u * dec_seq, (u + 1) * dec_seq)
            outs.append(one_seq(p * per_group + u, q8[rows], k8[rows], v8[rows]))
        att_scr[pl.ds(r, SUBLANES), :] = jnp.concatenate(outs, axis=0)
        return carry

    lax.fori_loop(0, ck_ref.shape[0] // per_group, per_group_of_rows, 0)
    y2_ref[...] = x + _dot(att_scr[...].astype(BF16), wo_ref[...]) + bo_ref[...]


def _swa_sample_call(y, ck, cv, dec_seq, g, w_qkv, b_qkv, qg, kg, ones_bd, bias_c, bias_n, sink,
                     w_o, b_o, n_kv, q_per_kv, head_dim):
    t, d = y.shape
    dec_batch, cw, n_k = ck.shape
    sb = min(SAMPLE_SEQ_BLOCK, dec_batch)
    rb = sb * dec_seq
    n_q = n_kv * q_per_kv * head_dim
    const = lambda shape: pl.BlockSpec(shape, lambda i: (0,) * len(shape))
    row_spec = pl.BlockSpec((rb, d), lambda i: (i, 0))
    cache_spec = pl.BlockSpec((sb, cw, n_k), lambda i: (i, 0, 0))
    return pl.pallas_call(
        functools.partial(_swa_sample_body, n_kv, q_per_kv, head_dim, dec_seq),
        grid=(dec_batch // sb,),
        in_specs=[row_spec, cache_spec, cache_spec, const((1, d)), const(w_qkv.shape),
                  const(b_qkv.shape), const(qg.shape), const(kg.shape), const(ones_bd.shape),
                  const(bias_c.shape), const(bias_n.shape), const(sink.shape), const(w_o.shape),
                  const((1, d))],
        out_specs=[row_spec, cache_spec, cache_spec],
        out_shape=[jax.ShapeDtypeStruct((t, d), F32),
                   jax.ShapeDtypeStruct(ck.shape, F32), jax.ShapeDtypeStruct(cv.shape, F32)],
        scratch_shapes=[pltpu.VMEM((rb, n_q), F32), pltpu.VMEM((rb, n_k), F32),
                        pltpu.VMEM((rb, n_k), F32), pltpu.VMEM((rb, n_q), F32)],
        input_output_aliases={0: 0},
        compiler_params=_params("arbitrary"),
        name="swa_sample",
    )(y, ck, cv, g, w_qkv, b_qkv, qg, kg, ones_bd, bias_c, bias_n, sink, w_o, b_o)


def _route_body(y_ref, g_ref, wr_ref, br_ref, ids_ref, gates_ref, rank_ref, tot_ref, carry):
    i = pl.program_id(0)

    @pl.when(i == 0)
    def _():
        carry[...] = jnp.zeros_like(carry)

    xn = _rms(y_ref[...], g_ref[...])
    logits = lax.dot_general(wr_ref[...], xn, (((1,), (1,)), ((), ())),
                             precision=lax.Precision.HIGHEST, preferred_element_type=F32)
    logits = logits + br_ref[:, 0:1]
    n_exp, tb = logits.shape
    e_iota = lax.broadcasted_iota(jnp.int32, (n_exp, tb), 0)
    work = logits
    vals, ids, hots = [], [], []
    for _ in range(TOP_K):
        m = jnp.max(work, axis=0, keepdims=True)
        idx = jnp.min(jnp.where(work == m, e_iota, n_exp), axis=0, keepdims=True)
        hot = e_iota == idx
        vals.append(m)
        ids.append(idx)
        hots.append(hot)
        work = jnp.where(hot, -jnp.inf, work)
    ex = [jnp.exp(v - vals[0]) for v in vals]
    den = ex[0] + ex[1] + ex[2] + ex[3]
    ids_ref[...] = jnp.concatenate(ids, axis=0)
    gates_ref[...] = jnp.concatenate([e / den for e in ex], axis=0)

    hot_all = jnp.zeros((n_exp, tb), F32)
    for hot in hots:
        hot_all = hot_all + hot.astype(F32)
    before = (lax.broadcasted_iota(jnp.int32, (tb, tb), 0)
              < lax.broadcasted_iota(jnp.int32, (tb, tb), 1)).astype(BF16)
    count = _dot(hot_all.astype(BF16), before) + carry[:, 0:1]
    rank_ref[...] = jnp.concatenate(
        [jnp.sum(jnp.where(hot, count, 0.0), axis=0, keepdims=True) for hot in hots],
        axis=0).astype(jnp.int32)
    total = carry[...] + jnp.sum(hot_all, axis=1, keepdims=True)
    carry[...] = total
    tot_ref[...] = total.astype(jnp.int32)


def _route_call(y, g, w_r_t, b_r):
    t, d = y.shape
    n_exp = w_r_t.shape[0]
    tb = ROUTE_BLOCK
    const = lambda shape: pl.BlockSpec(shape, lambda i: (0,) * len(shape))
    tok = lambda dtype: jax.ShapeDtypeStruct((TOP_K, t), dtype)
    tok_spec = pl.BlockSpec((TOP_K, tb), lambda i: (0, i))
    return pl.pallas_call(
        _route_body,
        grid=(t // tb,),
        in_specs=[pl.BlockSpec((tb, d), lambda i: (i, 0)), const((1, d)), const((n_exp, d)),
                  const((n_exp, LANES))],
        out_specs=[tok_spec, tok_spec, tok_spec, const((n_exp, LANES))],
        out_shape=[tok(jnp.int32), tok(F32), tok(jnp.int32),
                   jax.ShapeDtypeStruct((n_exp, LANES), jnp.int32)],
        scratch_shapes=[pltpu.VMEM((n_exp, LANES), F32)],
        compiler_params=_params("arbitrary"),
        name="moe_route",
    )(y, g, w_r_t, b_r)


def _store_row_tiles(ref, x):
    n = x.shape[0]
    for s in range(SUBLANES):
        ref[pl.ds(s, n, stride=SUBLANES), :] = x[:, s * LANES:(s + 1) * LANES]


def _load_row_tiles(ref, n):
    return jnp.concatenate([ref[pl.ds(s, n, stride=SUBLANES), :] for s in range(SUBLANES)], axis=1)


def _row_tile(ref, row):
    return ref.at[pl.ds(pl.multiple_of(row * SUBLANES, SUBLANES), SUBLANES)]


def _dispatch_body(tb, tile, ztile_ref, pos_ref, y_ref, g_ref, xs_hbm, xn_scr, zeros, sem, zsem):
    i = pl.program_id(0)
    tile8 = tile * SUBLANES

    @pl.when(i == 0)
    def _():
        zeros[...] = jnp.zeros_like(zeros)

        def fill(e):
            start = pl.multiple_of(ztile_ref[e] * SUBLANES, tile8)
            return pltpu.make_async_copy(zeros, xs_hbm.at[pl.ds(start, tile8)], zsem)

        for e in range(ztile_ref.shape[0]):
            @pl.when(ztile_ref[e] >= 0)
            def _():
                fill(e).start()
        for e in range(ztile_ref.shape[0]):
            @pl.when(ztile_ref[e] >= 0)
            def _():
                fill(e).wait()

    _store_row_tiles(xn_scr, _rms(y_ref[...], g_ref[...]))

    def issue(t, carry):
        for k in range(TOP_K):
            pltpu.make_async_copy(_row_tile(xn_scr, t), _row_tile(xs_hbm, pos_ref[k, t]), sem).start()
        return carry

    lax.fori_loop(0, tb, issue, 0, unroll=8)
    for k in range(TOP_K):
        pltpu.make_async_copy(xn_scr, xs_hbm.at[pl.ds(0, tb * SUBLANES)], sem).wait()


def _dispatch_call(pos, ztile, y, g, n_rows_padded):
    t, d = y.shape
    tb = DISPATCH_BLOCK
    tile = EXPERT_TILE
    return pl.pallas_call(
        functools.partial(_dispatch_body, tb, tile),
        grid_spec=pltpu.PrefetchScalarGridSpec(
            num_scalar_prefetch=1,
            grid=(t // tb,),
            in_specs=[pl.BlockSpec((TOP_K, tb), lambda i, z: (0, i), memory_space=pltpu.SMEM),
                      pl.BlockSpec((tb, d), lambda i, z: (i, 0)),
                      pl.BlockSpec((1, d), lambda i, z: (0, 0))],
            out_specs=pl.BlockSpec(memory_space=pl.ANY),
            scratch_shapes=[pltpu.VMEM((tb * SUBLANES, LANES), F32),
                            pltpu.VMEM((tile * SUBLANES, LANES), F32),
                            pltpu.SemaphoreType.DMA, pltpu.SemaphoreType.DMA]),
        out_shape=jax.ShapeDtypeStruct((n_rows_padded * SUBLANES, LANES), F32),
        compiler_params=_params("arbitrary"),
        name="moe_dispatch",
    )(ztile, pos, y, g)


def _expert_body(tile, max_tiles, tiles_ref, start_ref, used_ref, xs_hbm, w1_ref, b1_ref, w2_ref,
                 b2_ref, sel_ref, ys_hbm, w1p, w2b, xbuf, obuf, zeros, sem_in, sem_out, zsem):
    e = pl.program_id(0)
    nt = tiles_ref[e]
    d_ff2 = w1p.shape[1]
    pair = 2 * LANES
    tile8 = tile * SUBLANES

    def rows_of(tile_index):
        start = pl.multiple_of((start_ref[e] + tile_index * tile) * SUBLANES, tile8)
        return pl.ds(start, tile8)

    def in_copy(t, slot):
        return pltpu.make_async_copy(xs_hbm.at[rows_of(t)], xbuf.at[slot], sem_in.at[slot])

    def out_copy(t, slot):
        return pltpu.make_async_copy(obuf.at[slot], ys_hbm.at[rows_of(t)], sem_out.at[slot])

    @pl.when(nt > 0)
    def _():
        in_copy(0, 0).start()
        sel = sel_ref[...]
        for c in range(d_ff2 // pair):
            blk = w1_ref[0, 0, :, c * pair:(c + 1) * pair].astype(BF16)
            w1p[:, c * pair:(c + 1) * pair] = _dot(blk, sel).astype(BF16)
        w2b[...] = w2_ref[0, 0].astype(BF16)

        def step(t, carry):
            slot = t % 2

            @pl.when(t + 1 < nt)
            def _():
                in_copy(t + 1, 1 - slot).start()

            in_copy(t, slot).wait()

            @pl.when(t >= 2)
            def _():
                out_copy(t - 2, slot).wait()

            x = _load_row_tiles(xbuf.at[slot], tile).astype(BF16)
            h = _dot(x, w1p[...]) + b1_ref[0, 0]
            acts = []
            for c in range(d_ff2 // pair):
                gate = jnp.minimum(h[:, c * pair:c * pair + LANES], SWIGLU_LIMIT)
                up = jnp.clip(h[:, c * pair + LANES:(c + 1) * pair], -SWIGLU_LIMIT, SWIGLU_LIMIT)
                acts.append((up + 1.0) * (gate * (1.0 / (1.0 + jnp.exp(-SWIGLU_ALPHA * gate)))))
            a = jnp.concatenate(acts, axis=1).astype(BF16)
            _store_row_tiles(obuf.at[slot], _dot(a, w2b[...]) + b2_ref[0, 0])
            out_copy(t, slot).start()
            return carry

        lax.fori_loop(0, nt, step, 0)

        @pl.when(nt >= 2)
        def _():
            out_copy(nt - 2, nt % 2).wait()

        out_copy(nt - 1, (nt - 1) % 2).wait()

    @pl.when(e == pl.num_programs(0) - 1)
    def _():
        zeros[...] = jnp.zeros_like(zeros)
        spare = max_tiles - used_ref[0]

        def fill(s):
            start = pl.multiple_of((used_ref[0] + s) * tile8, tile8)
            return pltpu.make_async_copy(zeros, ys_hbm.at[pl.ds(start, tile8)], zsem)

        def start_fill(s, carry):
            fill(s).start()
            return carry

        def wait_fill(s, carry):
            fill(s).wait()
            return carry

        lax.fori_loop(0, spare, start_fill, 0)
        lax.fori_loop(0, spare, wait_fill, 0)


def _expert_call(layer, tiles_e, row_start, n_tiles, xs, w1, b1p, w2, b2, sel):
    tile = EXPERT_TILE
    tile8 = tile * SUBLANES
    max_tiles = xs.shape[0] // tile8
    n_exp, d, d_ff2 = w1.shape[1], w1.shape[2], w1.shape[3]
    d_ff = w2.shape[2]
    exp4 = lambda e, *_: (layer, e, 0, 0)
    any_spec = pl.BlockSpec(memory_space=pl.ANY)
    return pl.pallas_call(
        functools.partial(_expert_body, tile, max_tiles),
        grid_spec=pltpu.PrefetchScalarGridSpec(
            num_scalar_prefetch=3,
            grid=(n_exp,),
            in_specs=[any_spec,
                      pl.BlockSpec((1, 1, d, d_ff2), exp4), pl.BlockSpec((1, 1, 1, d_ff2), exp4),
                      pl.BlockSpec((1, 1, d_ff, d), exp4), pl.BlockSpec((1, 1, 1, d), exp4),
                      pl.BlockSpec(sel.shape, lambda e, *_: (0, 0))],
            out_specs=any_spec,
            scratch_shapes=[pltpu.VMEM((d, d_ff2), BF16), pltpu.VMEM((d_ff, d), BF16),
                            pltpu.VMEM((2, tile8, LANES), F32), pltpu.VMEM((2, tile8, LANES), F32),
                            pltpu.VMEM((tile8, LANES), F32),
                            pltpu.SemaphoreType.DMA((2,)), pltpu.SemaphoreType.DMA((2,)),
                            pltpu.SemaphoreType.DMA]),
        out_shape=jax.ShapeDtypeStruct(xs.shape, F32),
        compiler_params=_params("arbitrary"),
        name="moe_expert",
    )(tiles_e, row_start, n_tiles, xs, w1, b1p, w2, b2, sel)


def _combine_body(tb, pos_ref, posn_ref, gates_ref, y_ref, ys_hbm, out_ref, buf, sem):
    i = pl.program_id(0)
    nsteps = pl.num_programs(0)
    slot = i % 2

    def issue(table, dst_slot):
        def one(t, carry):
            for k in range(TOP_K):
                pltpu.make_async_copy(_row_tile(ys_hbm, table[k, t]),
                                      _row_tile(buf.at[dst_slot, k], t), sem.at[dst_slot]).start()
            return carry
        lax.fori_loop(0, tb, one, 0, unroll=8)

    @pl.when(i == 0)
    def _():
        issue(pos_ref, 0)

    @pl.when(i + 1 < nsteps)
    def _():
        issue(posn_ref, 1 - slot)

    for k in range(TOP_K):
        pltpu.make_async_copy(ys_hbm.at[pl.ds(0, tb * SUBLANES)], buf.at[slot, k], sem.at[slot]).wait()
    gate = [jnp.broadcast_to(gates_ref[:, k:k + 1], (tb, LANES)) for k in range(TOP_K)]
    for s in range(SUBLANES):
        cols = slice(s * LANES, (s + 1) * LANES)
        acc = y_ref[:, cols]
        for k in range(TOP_K):
            acc = acc + gate[k] * buf[slot, k, pl.ds(s, tb, stride=SUBLANES), :]
        out_ref[:, cols] = acc


def _combine_call(pos, gates_t, y, ys):
    t, d = y.shape
    tb = COMBINE_BLOCK
    nb = t // tb
    smem = lambda fn: pl.BlockSpec((TOP_K, tb), fn, memory_space=pltpu.SMEM)
    return pl.pallas_call(
        functools.partial(_combine_body, tb),
        grid=(nb,),
        in_specs=[smem(lambda i: (0, i)), smem(lambda i: (0, jnp.minimum(i + 1, nb - 1))),
                  pl.BlockSpec((tb, TOP_K), lambda i: (i, 0)),
                  pl.BlockSpec((tb, d), lambda i: (i, 0)),
                  pl.BlockSpec(memory_space=pl.ANY)],
        out_specs=pl.BlockSpec((tb, d), lambda i: (i, 0)),
        out_shape=jax.ShapeDtypeStruct((t, d), F32),
        scratch_shapes=[pltpu.VMEM((2, TOP_K, tb * SUBLANES, LANES), F32),
                        pltpu.SemaphoreType.DMA((2,))],
        compiler_params=_params("arbitrary"),
        name="moe_combine",
    )(pos, pos, gates_t, y, ys)


def _moe(layer, y, g, w_r, b_r, w1, b1p, w2, b2, sel):
    t, d = y.shape
    n_exp = w_r.shape[1]
    tile = EXPERT_TILE
    max_tiles = (TOP_K * t) // tile + n_exp
    b_r_l = jnp.broadcast_to(b_r[:, None], (n_exp, LANES))
    ids, gates, rank, totals = _route_call(y, g, w_r.T, b_r_l)

    n_e = totals[:, 0]
    tiles_e = (n_e + tile - 1) // tile
    tile_end = jnp.cumsum(tiles_e)
    row_start = (tile_end - tiles_e) * tile
    n_tiles = tile_end[-1:]
    pos = rank
    for e in range(n_exp):
        pos = pos + jnp.where(ids == e, row_start[e], 0)
    spare = n_tiles[0] + jnp.arange(n_exp, dtype=jnp.int32)
    ztile = jnp.concatenate([jnp.where(n_e % tile != 0, (tile_end - 1) * tile, -1),
                             jnp.where(spare < max_tiles, spare * tile, -1)]).astype(jnp.int32)

    xs = _dispatch_call(pos, ztile, y, g, max_tiles * tile)
    ys = _expert_call(layer, tiles_e.astype(jnp.int32), row_start.astype(jnp.int32),
                      n_tiles.astype(jnp.int32), xs, w1, b1p, w2, b2, sel)
    return _combine_call(pos, gates.T, y, ys)


def _t5_bucket(dist):
    n = jnp.maximum(dist, 0)
    max_exact = NUM_BUCKETS // 2
    nf = jnp.maximum(n, max_exact).astype(F32)
    large = max_exact + (jnp.log(nf / max_exact) / math.log(MAX_DISTANCE / max_exact)
                         * (NUM_BUCKETS - max_exact)).astype(jnp.int32)
    large = jnp.minimum(large, NUM_BUCKETS - 1)
    return jnp.where(n < max_exact, n, large)


def _window_bias(dist, rel_bias, n_kv, q_per_kv):
    tq, tk = dist.shape
    hot = jax.nn.one_hot(_t5_bucket(dist), NUM_BUCKETS, dtype=F32)
    bias = jnp.einsum("qkb,bh->hqk", hot, rel_bias.astype(F32), precision=lax.Precision.HIGHEST)
    valid = (dist >= 0) & (dist < WINDOW)
    return jnp.where(valid, bias, NEG_INF).reshape(n_kv, q_per_kv * tq, tk)


def _selection_matrix():
    src = jnp.arange(2 * LANES)
    dst = (src % 2) * LANES + src // 2
    return jax.nn.one_hot(dst, 2 * LANES, dtype=BF16)


def kernel(x_prompt, x_sample, cache_k, cache_v, norm_mix_g, norm_ffn_g, a_w_in, a_b_in, a_ln_g,
           a_ln_b, a_w_s, a_b_s, a_w_out, a_b_out, s_w_qkv, s_b_qkv, s_q_norm_g, s_k_norm_g,
           s_sinks, s_w_o, s_b_o, rel_bias, m_w_router, m_b_router, m_w1, m_b1, m_w2, m_b2):
    batch, seq, d = x_prompt.shape
    dec_batch, dec_seq, _ = x_sample.shape
    depth = norm_mix_g.shape[0]
    n_groups = a_w_s.shape[1]
    n_heads = s_sinks.shape[1]
    n_kv, head_dim = cache_k.shape[3], cache_k.shape[4]
    q_per_kv = n_heads // n_kv
    cw = cache_k.shape[2]
    ns = dec_batch * dec_seq
    assert d == SUBLANES * LANES, "the routed rows are moved as one (8, 128) tile per token"
    assert ns % ROW_BLOCK == 0 and seq % ROW_BLOCK == 0 and CHUNK % dec_seq == 0
    assert (ns + batch * seq) % ROUTE_BLOCK == 0 and seq >= WINDOW and SUBLANES % dec_seq == 0

    y = jnp.concatenate([x_sample.reshape(ns, d), x_prompt.reshape(batch * seq, d)], axis=0)
    sel = _selection_matrix()
    ones_bd = jnp.kron(jnp.eye(LANES // head_dim, dtype=F32),
                       jnp.ones((head_dim, head_dim), F32)).astype(BF16)

    qi, kj = jnp.arange(WINDOW), jnp.arange(2 * WINDOW)
    bias_p = _window_bias(WINDOW + qi[:, None] - kj[None, :], rel_bias, n_kv, q_per_kv)
    qt = jnp.arange(dec_seq)
    bias_c = _window_bias(qt[:, None] + cw - jnp.arange(cw)[None, :], rel_bias, n_kv, q_per_kv)
    tn = jnp.arange(SUBLANES)
    dist_n = jnp.where(tn[None, :] < dec_seq, qt[:, None] - tn[None, :], -1)
    bias_n = _window_bias(dist_n, rel_bias, n_kv, q_per_kv)

    n_exp, d_ff2 = m_b1.shape[1], m_b1.shape[2]
    b1p = (m_b1.reshape(depth, n_exp, d_ff2 // (2 * LANES), LANES, 2).swapaxes(3, 4)
           .reshape(depth, n_exp, 1, d_ff2))
    b2 = m_b2[:, :, None, :]

    row = lambda a: a.reshape(1, -1)
    new_kp, new_vp, new_ks, new_vs, new_va = [], [], [], [], []
    for i in range(depth):
        j = i // 2
        g_mix = row(norm_mix_g[i])
        if i % 2 == 0:
            w_s = jnp.tril(a_w_s[j])
            c = min(dec_seq, CHUNK)
            w_sample = jnp.einsum("ab,gpq->gapbq", jnp.eye(CHUNK // c, dtype=F32),
                                  jnp.tril(a_w_s[j][:, :c, :c])).reshape(n_groups, CHUNK, CHUNK)
            wsp = jnp.stack([w_sample, w_s[:, :CHUNK, :CHUNK]]).astype(BF16)
            dg = a_w_out.shape[1] // n_groups
            b_prompt = jnp.repeat(a_b_s[j][:, :CHUNK].T, dg, axis=1)
            b_sample = jnp.repeat(jnp.tile(a_b_s[j][:, :c].T, (CHUNK // c, 1)), dg, axis=1)
            bsp = jnp.stack([b_sample, b_prompt])
            y, v_s = _sgu_call(y, ns, g_mix, a_w_in[j].astype(BF16), row(a_b_in[j]), row(a_ln_g[j]),
                               row(a_ln_b[j]), wsp, bsp, a_w_out[j].astype(BF16), row(a_b_out[j]))
            new_va.append(v_s.reshape(dec_batch, dec_seq, -1))
        else:
            w_qkv, b_qkv = s_w_qkv[j].astype(BF16), row(s_b_qkv[j])
            qg = row(jnp.tile(s_q_norm_g[j], n_heads))
            kg = row(jnp.tile(s_k_norm_g[j], n_kv))
            w_o, b_o = s_w_o[j].astype(BF16), row(s_b_o[j])
            sinks = s_sinks[j].astype(F32).reshape(n_kv, q_per_kv, 1)
            sink_p = jnp.repeat(sinks, WINDOW, axis=1)
            sink_s = jnp.repeat(sinks, dec_seq, axis=1)
            ck = cache_k[j].reshape(dec_batch, cw, n_kv * head_dim)
            cv = cache_v[j].reshape(dec_batch, cw, n_kv * head_dim)
            y, ks, vs = _swa_sample_call(y, ck, cv, dec_seq, g_mix, w_qkv, b_qkv, qg, kg, ones_bd,
                                         bias_c, bias_n, sink_s, w_o, b_o, n_kv, q_per_kv, head_dim)
            y, kp, vp = _swa_prompt_call(y, ns, batch, seq, g_mix, w_qkv, b_qkv, qg, kg, ones_bd,
                                         bias_p, sink_p, w_o, b_o, n_kv, q_per_kv, head_dim)
            new_ks.append(ks.reshape(dec_batch, cw, n_kv, head_dim))
            new_vs.append(vs.reshape(dec_batch, cw, n_kv, head_dim))
            new_kp.append(kp.reshape(batch, WINDOW, n_kv, head_dim))
            new_vp.append(vp.reshape(batch, WINDOW, n_kv, head_dim))
        y = _moe(i, y, row(norm_ffn_g[i]), m_w_router[i], m_b_router[i], m_w1, b1p, m_w2, b2, sel)
    yp = y[ns:].reshape(batch, seq, d)
    ys = y[:ns].reshape(dec_batch, dec_seq, d)
    return (yp, ys, jnp.stack(new_kp), jnp.stack(new_vp), jnp.stack(new_ks), jnp.stack(new_vs),
            jnp.stack(new_va))
```

```python
import functools
import math

import jax
import jax.numpy as jnp
from jax import lax
from jax.experimental import pallas as pl
from jax.experimental.pallas import tpu as pltpu

EPS = 1e-6
NEG_INF = -1e30
CHUNK = 128
WINDOW = 128
NUM_BUCKETS = 32
MAX_DISTANCE = 128
TOP_K = 4
SWIGLU_LIMIT = 7.0
SWIGLU_ALPHA = 1.702

LANES = 128
SUBLANES = 8
VMEM_LIMIT_BYTES = 56 << 20
DMA_THREADS = 2

ROW_BLOCK = 256
ROUTE_BLOCK = 512
EXPERT_TILE = 256
DISPATCH_BLOCK = 512
COMBINE_BLOCK = 256
SAMPLE_SEQ_BLOCK = 32

F32 = jnp.float32
BF16 = jnp.bfloat16


def _params(*sem):
    return pltpu.CompilerParams(dimension_semantics=sem, vmem_limit_bytes=VMEM_LIMIT_BYTES)


def _rms(x, g):
    return x * lax.rsqrt(jnp.mean(x * x, axis=-1, keepdims=True) + EPS) * g


def _dot(a, b):
    return jnp.dot(a, b, preferred_element_type=F32)


def _dot_nt(a, b):
    return lax.dot_general(a, b, (((1,), (1,)), ((), ())), preferred_element_type=F32)


def _sgu_body(n_sample_blocks, n_groups, y_ref, g_ref, win_ref, bin_ref, lng_ref, lnb_ref,
              wsp_ref, bsp_ref, wout_ref, bout_ref, y2_ref, v_ref):
    i = pl.program_id(0)
    x = y_ref[...]
    xn = _rms(x, g_ref[...]).astype(BF16)
    z = _dot(xn, win_ref[...]) + bin_ref[...]
    z = 0.5 * z * (1.0 + lax.erf(z * (1.0 / math.sqrt(2.0))))
    half = z.shape[1] // 2
    u, v = z[:, :half], z[:, half:]
    mu = jnp.mean(v, axis=-1, keepdims=True)
    vc = v - mu
    vn = vc * lax.rsqrt(jnp.mean(vc * vc, axis=-1, keepdims=True) + EPS) * lng_ref[...] + lnb_ref[...]

    @pl.when(i < n_sample_blocks)
    def _():
        v_ref[...] = vn

    vb = vn.astype(BF16)
    dg = half // n_groups
    rows = []
    for c in range(x.shape[0] // CHUNK):
        r0 = c * CHUNK
        cols = []
        for g in range(n_groups):
            gate = _dot(wsp_ref[0, g], vb[r0:r0 + CHUNK, g * dg:(g + 1) * dg])
            gate = gate + bsp_ref[0, :, g * dg:(g + 1) * dg]
            cols.append(u[r0:r0 + CHUNK, g * dg:(g + 1) * dg] * gate)
        rows.append(jnp.concatenate(cols, axis=1))
    out = jnp.concatenate(rows, axis=0).astype(BF16)
    y2_ref[...] = x + _dot(out, wout_ref[...]) + bout_ref[...]


def _sgu_call(y, n_sample_rows, g, w_in, b_in, ln_g, ln_b, wsp, bsp, w_out, b_out):
    t, d = y.shape
    rb = ROW_BLOCK
    nsb = n_sample_rows // rb
    n_groups = wsp.shape[1]
    half = w_out.shape[0]
    const = lambda shape: pl.BlockSpec(shape, lambda i: (0,) * len(shape))
    which = lambda i: jnp.where(i < nsb, 0, 1)
    return pl.pallas_call(
        functools.partial(_sgu_body, nsb, n_groups),
        grid=(t // rb,),
        in_specs=[
            pl.BlockSpec((rb, d), lambda i: (i, 0)),
            const((1, d)), const(w_in.shape), const((1, 2 * half)), const((1, half)), const((1, half)),
            pl.BlockSpec((1, n_groups, CHUNK, CHUNK), lambda i: (which(i), 0, 0, 0)),
            pl.BlockSpec((1, CHUNK, half), lambda i: (which(i), 0, 0)),
            const(w_out.shape), const((1, d)),
        ],
        out_specs=[
            pl.BlockSpec((rb, d), lambda i: (i, 0)),
            pl.BlockSpec((rb, half), lambda i: (jnp.minimum(i, nsb - 1), 0)),
        ],
        out_shape=[jax.ShapeDtypeStruct((t, d), F32), jax.ShapeDtypeStruct((n_sample_rows, half), F32)],
        compiler_params=_params("arbitrary"),
        name="sgu",
    )(y, g, w_in, b_in, ln_g, ln_b, wsp, bsp, w_out, b_out)


def _head_mean_square(z, ones_bd, head_dim):
    sq = z * z
    tiles = z.shape[1] // LANES
    st = jnp.concatenate([sq[:, c * LANES:(c + 1) * LANES] for c in range(tiles)], axis=0)
    hi = st.astype(BF16)
    lo = (st - hi.astype(F32)).astype(BF16)
    ms = (_dot(hi, ones_bd) + _dot(lo, ones_bd)) * (1.0 / head_dim)
    r = z.shape[0]
    return jnp.concatenate([ms[c * r:(c + 1) * r] for c in range(tiles)], axis=1)


def _qkv(x, g_ref, wqkv_ref, bqkv_ref, qg_ref, kg_ref, ones_ref, n_q, n_k, head_dim):
    xn = _rms(x, g_ref[...]).astype(BF16)
    qkv = _dot(xn, wqkv_ref[...]) + bqkv_ref[...]
    q, k, v = qkv[:, :n_q], qkv[:, n_q:n_q + n_k], qkv[:, n_q + n_k:]
    ones_bd = ones_ref[...]
    q = q * lax.rsqrt(_head_mean_square(q, ones_bd, head_dim) + EPS) * qg_ref[...]
    k = k * lax.rsqrt(_head_mean_square(k, ones_bd, head_dim) + EPS) * kg_ref[...]
    return q, k, v


def _swa_prompt_body(n_kv, q_per_kv, head_dim, y_ref, g_ref, wqkv_ref, bqkv_ref, qg_ref, kg_ref,
                     ones_ref, bias_ref, sink_ref, wo_ref, bo_ref, y2_ref, ko_ref, vo_ref,
                     kprev, vprev):
    n = pl.program_id(1)
    n_q, n_k = n_kv * q_per_kv * head_dim, n_kv * head_dim
    x = y_ref[...]
    q, k, v = _qkv(x, g_ref, wqkv_ref, bqkv_ref, qg_ref, kg_ref, ones_ref, n_q, n_k, head_dim)
    qb = (q * (head_dim ** -0.5)).astype(BF16)

    @pl.when(n == 0)
    def _():
        kprev[...] = jnp.zeros_like(kprev)
        vprev[...] = jnp.zeros_like(vprev)

    w = WINDOW
    low_half = lax.broadcasted_iota(jnp.int32, (w, LANES), 1) < head_dim

    def lo_hi(z, j):
        tile = z[:, (j // 2) * LANES:(j // 2 + 1) * LANES]
        same = jnp.where(low_half == (j % 2 == 0), tile, 0.0)
        other = pltpu.roll(same, head_dim, 1)
        lo, hi = (same, other) if j % 2 == 0 else (other, same)
        return lo.astype(BF16), hi.astype(BF16)

    pairs = q_per_kv // 2
    blk = pairs * w
    col = lax.broadcasted_iota(jnp.int32, (2 * n_kv * blk, 2 * w), 1)
    no_prev = jnp.logical_and(n == 0, col < w)
    sink = sink_ref[...]
    k_prev = [(kprev[j, 0], kprev[j, 1]) for j in range(n_kv)]
    v_prev = [(vprev[j, 0], vprev[j, 1]) for j in range(n_kv)]
    att_rows = []
    for c in range(x.shape[0] // w):
        rows = slice(c * w, (c + 1) * w)
        scores, values = [], []
        for j in range(n_kv):
            k_cur, v_cur = lo_hi(k[rows], j), lo_hi(v[rows], j)
            qj = jnp.concatenate([qb[rows, (j * pairs + a) * LANES:(j * pairs + a + 1) * LANES]
                                  for a in range(pairs)], axis=0)
            for half in range(2):
                scores.append(_dot_nt(qj, jnp.concatenate([k_prev[j][half], k_cur[half]], axis=0)))
            values.append(jnp.concatenate([v_prev[j][0], v_cur[0], v_prev[j][1], v_cur[1]], axis=0))
            k_prev[j], v_prev[j] = k_cur, v_cur
        bias = bias_ref[...]
        if c == 0:
            bias = jnp.where(no_prev, NEG_INF, bias)
        s = jnp.concatenate(scores, axis=0) + bias
        m = jnp.maximum(jnp.max(s, axis=1, keepdims=True), sink)
        p = jnp.exp(s - m)
        den = jnp.sum(p, axis=1, keepdims=True) + jnp.exp(sink - m)
        p = (p * (1.0 / den)).astype(BF16)
        tiles = []
        for j in range(n_kv):
            probs = jnp.concatenate([p[(2 * j) * blk:(2 * j + 1) * blk],
                                     p[(2 * j + 1) * blk:(2 * j + 2) * blk]], axis=1)
            o = _dot(probs, values[j])
            tiles.extend(o[a * w:(a + 1) * w] for a in range(pairs))
        att_rows.append(jnp.concatenate(tiles, axis=1))
    att = jnp.concatenate(att_rows, axis=0).astype(BF16)
    y2_ref[...] = x + _dot(att, wo_ref[...]) + bo_ref[...]
    for j in range(n_kv):
        for half in range(2):
            kprev[j, half] = k_prev[j][half]
            vprev[j, half] = v_prev[j][half]
    last = x.shape[0] - w

    @pl.when(n == pl.num_programs(1) - 1)
    def _():
        ko_ref[0] = k[last:]
        vo_ref[0] = v[last:]


def _swa_prompt_call(y, n_sample_rows, batch, seq, g, w_qkv, b_qkv, qg, kg, ones_bd, bias, sink,
                     w_o, b_o, n_kv, q_per_kv, head_dim):
    t, d = y.shape
    rb = ROW_BLOCK
    nb = seq // rb
    base = n_sample_rows // rb
    n_k = n_kv * head_dim
    const = lambda shape: pl.BlockSpec(shape, lambda b, n: (0,) * len(shape))
    row_spec = pl.BlockSpec((rb, d), lambda b, n: (base + b * nb + n, 0))
    kv_spec = pl.BlockSpec((1, WINDOW, n_k), lambda b, n: (b, 0, 0))
    return pl.pallas_call(
        functools.partial(_swa_prompt_body, n_kv, q_per_kv, head_dim),
        grid=(batch, nb),
        in_specs=[row_spec, const((1, d)), const(w_qkv.shape), const(b_qkv.shape), const(qg.shape),
                  const(kg.shape), const(ones_bd.shape), const(bias.shape), const(sink.shape),
                  const(w_o.shape), const((1, d))],
        out_specs=[row_spec, kv_spec, kv_spec],
        out_shape=[jax.ShapeDtypeStruct((t, d), F32),
                   jax.ShapeDtypeStruct((batch, WINDOW, n_k), F32),
                   jax.ShapeDtypeStruct((batch, WINDOW, n_k), F32)],
        scratch_shapes=[pltpu.VMEM((n_kv, 2, WINDOW, LANES), BF16),
                        pltpu.VMEM((n_kv, 2, WINDOW, LANES), BF16)],
        input_output_aliases={0: 0},
        compiler_params=_params("arbitrary", "arbitrary"),
        name="swa_prompt",
    )(y, g, w_qkv, b_qkv, qg, kg, ones_bd, bias, sink, w_o, b_o)


def _swa_sample_body(n_kv, q_per_kv, head_dim, dec_seq, y_ref, ck_ref, cv_ref, g_ref, wqkv_ref,
                     bqkv_ref, qg_ref, kg_ref, ones_ref, biasc_ref, biasn_ref, sink_ref, wo_ref,
                     bo_ref, y2_ref, ko_ref, vo_ref, q_scr, k_scr, v_scr, att_scr):
    n_q, n_k = n_kv * q_per_kv * head_dim, n_kv * head_dim
    x = y_ref[...]
    q, k, v = _qkv(x, g_ref, wqkv_ref, bqkv_ref, qg_ref, kg_ref, ones_ref, n_q, n_k, head_dim)
    q_scr[...] = q * (head_dim ** -0.5)
    k_scr[...] = k
    v_scr[...] = v
    cw = ck_ref.shape[1]
    pad = jnp.zeros((SUBLANES - dec_seq, n_k), F32)

    def one_seq(s, qs, k_new, v_new):
        ck, cv = ck_ref[s], cv_ref[s]
        ko_ref[s, 0:cw - dec_seq, :] = ck[dec_seq:]
        ko_ref[s, cw - dec_seq:cw, :] = k_new
        vo_ref[s, 0:cw - dec_seq, :] = cv[dec_seq:]
        vo_ref[s, cw - dec_seq:cw, :] = v_new
        qs = qs.astype(BF16)
        ckb, cvb = ck.astype(BF16), cv.astype(BF16)
        kn8 = jnp.concatenate([k_new, pad], axis=0).astype(BF16)
        vn8 = jnp.concatenate([v_new, pad], axis=0).astype(BF16)
        outs = []
        for j in range(n_kv):
            lo, hi = j * head_dim, (j + 1) * head_dim
            qj = jnp.concatenate(
                [qs[:, (j * q_per_kv + g) * head_dim:(j * q_per_kv + g + 1) * head_dim]
                 for g in range(q_per_kv)], axis=0)
            s_c = _dot_nt(qj, ckb[:, lo:hi]) + biasc_ref[j]
            s_n = _dot_nt(qj, kn8[:, lo:hi]) + biasn_ref[j]
            sink = sink_ref[j]
            m = jnp.maximum(jnp.maximum(jnp.max(s_c, axis=1, keepdims=True),
                                        jnp.max(s_n, axis=1, keepdims=True)), sink)
            p_c, p_n = jnp.exp(s_c - m), jnp.exp(s_n - m)
            den = (jnp.sum(p_c, axis=1, keepdims=True) + jnp.sum(p_n, axis=1, keepdims=True)
                   + jnp.exp(sink - m))
            o = (_dot(p_c.astype(BF16), cvb[:, lo:hi]) + _dot(p_n.astype(BF16), vn8[:, lo:hi])) / den
            outs.extend(o[g * dec_seq:(g + 1) * dec_seq] for g in range(q_per_kv))
        return jnp.concatenate(outs, axis=1)

    per_group = SUBLANES // dec_seq

    def per_group_of_rows(p, carry):
        r = pl.multiple_of(p * SUBLANES, SUBLANES)
        q8, k8, v8 = q_scr[pl.ds(r, SUBLANES), :], k_scr[pl.ds(r, SUBLANES), :], v_scr[pl.ds(r, SUBLANES), :]
        outs = []
        for u in range(per_group):
            rows = slice(u * dec_seq, (u + 1) * dec_seq)
            outs.append(one_seq(p * per_group + u, q8[rows], k8[rows], v8[rows]))
        att_scr[pl.ds(r, SUBLANES), :] = jnp.concatenate(outs, axis=0)
        return carry

    lax.fori_loop(0, ck_ref.shape[0] // per_group, per_group_of_rows, 0)
    y2_ref[...] = x + _dot(att_scr[...].astype(BF16), wo_ref[...]) + bo_ref[...]


def _swa_sample_call(y, ck, cv, dec_seq, g, w_qkv, b_qkv, qg, kg, ones_bd, bias_c, bias_n, sink,
                     w_o, b_o, n_kv, q_per_kv, head_dim):
    t, d = y.shape
    dec_batch, cw, n_k = ck.shape
    sb = min(SAMPLE_SEQ_BLOCK, dec_batch)
    rb = sb * dec_seq
    n_q = n_kv * q_per_kv * head_dim
    const = lambda shape: pl.BlockSpec(shape, lambda i: (0,) * len(shape))
    row_spec = pl.BlockSpec((rb, d), lambda i: (i, 0))
    cache_spec = pl.BlockSpec((sb, cw, n_k), lambda i: (i, 0, 0))
    return pl.pallas_call(
        functools.partial(_swa_sample_body, n_kv, q_per_kv, head_dim, dec_seq),
        grid=(dec_batch // sb,),
        in_specs=[row_spec, cache_spec, cache_spec, const((1, d)), const(w_qkv.shape),
                  const(b_qkv.shape), const(qg.shape), const(kg.shape), const(ones_bd.shape),
                  const(bias_c.shape), const(bias_n.shape), const(sink.shape), const(w_o.shape),
                  const((1, d))],
        out_specs=[row_spec, cache_spec, cache_spec],
        out_shape=[jax.ShapeDtypeStruct((t, d), F32),
                   jax.ShapeDtypeStruct(ck.shape, F32), jax.ShapeDtypeStruct(cv.shape, F32)],
        scratch_shapes=[pltpu.VMEM((rb, n_q), F32), pltpu.VMEM((rb, n_k), F32),
                        pltpu.VMEM((rb, n_k), F32), pltpu.VMEM((rb, n_q), F32)],
        input_output_aliases={0: 0},
        compiler_params=_params("arbitrary"),
        name="swa_sample",
    )(y, ck, cv, g, w_qkv, b_qkv, qg, kg, ones_bd, bias_c, bias_n, sink, w_o, b_o)


def _route_body(y_ref, g_ref, wr_ref, br_ref, ids_ref, gates_ref, rank_ref, tot_ref, carry):
    i = pl.program_id(0)

    @pl.when(i == 0)
    def _():
        carry[...] = jnp.zeros_like(carry)

    xn = _rms(y_ref[...], g_ref[...])
    logits = lax.dot_general(wr_ref[...], xn, (((1,), (1,)), ((), ())),
                             precision=lax.Precision.HIGHEST, preferred_element_type=F32)
    logits = logits + br_ref[:, 0:1]
    n_exp, tb = logits.shape
    e_iota = lax.broadcasted_iota(jnp.int32, (n_exp, tb), 0)
    work = logits
    vals, ids, hots = [], [], []
    for _ in range(TOP_K):
        m = jnp.max(work, axis=0, keepdims=True)
        idx = jnp.min(jnp.where(work == m, e_iota, n_exp), axis=0, keepdims=True)
        hot = e_iota == idx
        vals.append(m)
        ids.append(idx)
        hots.append(hot)
        work = jnp.where(hot, -jnp.inf, work)
    ex = [jnp.exp(v - vals[0]) for v in vals]
    den = ex[0] + ex[1] + ex[2] + ex[3]
    ids_ref[...] = jnp.concatenate(ids, axis=0)
    gates_ref[...] = jnp.concatenate([e / den for e in ex], axis=0)

    hot_all = jnp.zeros((n_exp, tb), F32)
    for hot in hots:
        hot_all = hot_all + hot.astype(F32)
    before = (lax.broadcasted_iota(jnp.int32, (tb, tb), 0)
              < lax.broadcasted_iota(jnp.int32, (tb, tb), 1)).astype(BF16)
    count = _dot(hot_all.astype(BF16), before) + carry[:, 0:1]
    rank_ref[...] = jnp.concatenate(
        [jnp.sum(jnp.where(hot, count, 0.0), axis=0, keepdims=True) for hot in hots],
        axis=0).astype(jnp.int32)
    total = carry[...] + jnp.sum(hot_all, axis=1, keepdims=True)
    carry[...] = total
    tot_ref[...] = total.astype(jnp.int32)


def _route_call(y, g, w_r_t, b_r):
    t, d = y.shape
    n_exp = w_r_t.shape[0]
    tb = ROUTE_BLOCK
    const = lambda shape: pl.BlockSpec(shape, lambda i: (0,) * len(shape))
    tok = lambda dtype: jax.ShapeDtypeStruct((TOP_K, t), dtype)
    tok_spec = pl.BlockSpec((TOP_K, tb), lambda i: (0, i))
    return pl.pallas_call(
        _route_body,
        grid=(t // tb,),
        in_specs=[pl.BlockSpec((tb, d), lambda i: (i, 0)), const((1, d)), const((n_exp, d)),
                  const((n_exp, LANES))],
        out_specs=[tok_spec, tok_spec, tok_spec, const((n_exp, LANES))],
        out_shape=[tok(jnp.int32), tok(F32), tok(jnp.int32),
                   jax.ShapeDtypeStruct((n_exp, LANES), jnp.int32)],
        scratch_shapes=[pltpu.VMEM((n_exp, LANES), F32)],
        compiler_params=_params("arbitrary"),
        name="moe_route",
    )(y, g, w_r_t, b_r)


def _store_row_tiles(ref, x):
    n = x.shape[0]
    for s in range(SUBLANES):
        ref[pl.ds(s, n, stride=SUBLANES), :] = x[:, s * LANES:(s + 1) * LANES]


def _load_row_tiles(ref, n):
    return jnp.concatenate([ref[pl.ds(s, n, stride=SUBLANES), :] for s in range(SUBLANES)], axis=1)


def _row_tile(ref, row):
    return ref.at[pl.ds(pl.multiple_of(row * SUBLANES, SUBLANES), SUBLANES)]


def _dispatch_body(tb, tile, ztile_ref, pos_ref, y_ref, g_ref, xs_hbm, xn_scr, zeros, sem, zsem):
    i = pl.program_id(0)
    tile8 = tile * SUBLANES

    @pl.when(i == 0)
    def _():
        zeros[...] = jnp.zeros_like(zeros)

        def fill(e):
            start = pl.multiple_of(ztile_ref[e] * SUBLANES, tile8)
            return pltpu.make_async_copy(zeros, xs_hbm.at[pl.ds(start, tile8)], zsem)

        for e in range(ztile_ref.shape[0]):
            @pl.when(ztile_ref[e] >= 0)
            def _():
                fill(e).start()
        for e in range(ztile_ref.shape[0]):
            @pl.when(ztile_ref[e] >= 0)
            def _():
                fill(e).wait()

    _store_row_tiles(xn_scr, _rms(y_ref[...], g_ref[...]))

    def issue(t, carry):
        for k in range(TOP_K):
            pltpu.make_async_copy(_row_tile(xn_scr, t), _row_tile(xs_hbm, pos_ref[k, t]),
                                  sem).start(priority=k % DMA_THREADS)
        return carry

    lax.fori_loop(0, tb, issue, 0, unroll=8)
    for k in range(TOP_K):
        pltpu.make_async_copy(xn_scr, xs_hbm.at[pl.ds(0, tb * SUBLANES)], sem).wait()


def _dispatch_call(pos, ztile, y, g, n_rows_padded):
    t, d = y.shape
    tb = DISPATCH_BLOCK
    tile = EXPERT_TILE
    return pl.pallas_call(
        functools.partial(_dispatch_body, tb, tile),
        grid_spec=pltpu.PrefetchScalarGridSpec(
            num_scalar_prefetch=1,
            grid=(t // tb,),
            in_specs=[pl.BlockSpec((TOP_K, tb), lambda i, z: (0, i), memory_space=pltpu.SMEM),
                      pl.BlockSpec((tb, d), lambda i, z: (i, 0)),
                      pl.BlockSpec((1, d), lambda i, z: (0, 0))],
            out_specs=pl.BlockSpec(memory_space=pl.ANY),
            scratch_shapes=[pltpu.VMEM((tb * SUBLANES, LANES), F32),
                            pltpu.VMEM((tile * SUBLANES, LANES), F32),
                            pltpu.SemaphoreType.DMA, pltpu.SemaphoreType.DMA]),
        out_shape=jax.ShapeDtypeStruct((n_rows_padded * SUBLANES, LANES), F32),
        compiler_params=_params("arbitrary"),
        name="moe_dispatch",
    )(ztile, pos, y, g)


def _expert_body(tile, max_tiles, tiles_ref, start_ref, used_ref, xs_hbm, w1_ref, b1_ref, w2_ref,
                 b2_ref, sel_ref, ys_hbm, w1p, w2b, xbuf, obuf, zeros, sem_in, sem_out, zsem):
    e = pl.program_id(0)
    nt = tiles_ref[e]
    d_ff2 = w1p.shape[1]
    pair = 2 * LANES
    tile8 = tile * SUBLANES

    def rows_of(tile_index):
        start = pl.multiple_of((start_ref[e] + tile_index * tile) * SUBLANES, tile8)
        return pl.ds(start, tile8)

    def in_copy(t, slot):
        return pltpu.make_async_copy(xs_hbm.at[rows_of(t)], xbuf.at[slot], sem_in.at[slot])

    def out_copy(t, slot):
        return pltpu.make_async_copy(obuf.at[slot], ys_hbm.at[rows_of(t)], sem_out.at[slot])

    tile_thread = DMA_THREADS - 1

    @pl.when(nt > 0)
    def _():
        in_copy(0, 0).start(priority=tile_thread)
        sel = sel_ref[...]
        for c in range(d_ff2 // pair):
            blk = w1_ref[0, 0, :, c * pair:(c + 1) * pair].astype(BF16)
            w1p[:, c * pair:(c + 1) * pair] = _dot(blk, sel).astype(BF16)
        w2b[...] = w2_ref[0, 0].astype(BF16)

        def step(t, carry):
            slot = t % 2

            @pl.when(t + 1 < nt)
            def _():
                in_copy(t + 1, 1 - slot).start(priority=tile_thread)

            in_copy(t, slot).wait()

            @pl.when(t >= 2)
            def _():
                out_copy(t - 2, slot).wait()

            x = _load_row_tiles(xbuf.at[slot], tile).astype(BF16)
            h = _dot(x, w1p[...]) + b1_ref[0, 0]
            acts = []
            for c in range(d_ff2 // pair):
                gate = jnp.minimum(h[:, c * pair:c * pair + LANES], SWIGLU_LIMIT)
                up = jnp.clip(h[:, c * pair + LANES:(c + 1) * pair], -SWIGLU_LIMIT, SWIGLU_LIMIT)
                acts.append((up + 1.0) * (gate * (1.0 / (1.0 + jnp.exp(-SWIGLU_ALPHA * gate)))))
            a = jnp.concatenate(acts, axis=1).astype(BF16)
            _store_row_tiles(obuf.at[slot], _dot(a, w2b[...]) + b2_ref[0, 0])
            out_copy(t, slot).start(priority=tile_thread)
            return carry

        lax.fori_loop(0, nt, step, 0)

        @pl.when(nt >= 2)
        def _():
            out_copy(nt - 2, nt % 2).wait()

        out_copy(nt - 1, (nt - 1) % 2).wait()

    @pl.when(e == pl.num_programs(0) - 1)
    def _():
        zeros[...] = jnp.zeros_like(zeros)
        spare = max_tiles - used_ref[0]

        def fill(s):
            start = pl.multiple_of((used_ref[0] + s) * tile8, tile8)
            return pltpu.make_async_copy(zeros, ys_hbm.at[pl.ds(start, tile8)], zsem)

        def start_fill(s, carry):
            fill(s).start()
            return carry

        def wait_fill(s, carry):
            fill(s).wait()
            return carry

        lax.fori_loop(0, spare, start_fill, 0)
        lax.fori_loop(0, spare, wait_fill, 0)


def _expert_call(layer, tiles_e, row_start, n_tiles, xs, w1, b1p, w2, b2, sel):
    tile = EXPERT_TILE
    tile8 = tile * SUBLANES
    max_tiles = xs.shape[0] // tile8
    n_exp, d, d_ff2 = w1.shape[1], w1.shape[2], w1.shape[3]
    d_ff = w2.shape[2]
    exp4 = lambda e, *_: (layer, e, 0, 0)
    any_spec = pl.BlockSpec(memory_space=pl.ANY)
    return pl.pallas_call(
        functools.partial(_expert_body, tile, max_tiles),
        grid_spec=pltpu.PrefetchScalarGridSpec(
            num_scalar_prefetch=3,
            grid=(n_exp,),
            in_specs=[any_spec,
                      pl.BlockSpec((1, 1, d, d_ff2), exp4), pl.BlockSpec((1, 1, 1, d_ff2), exp4),
                      pl.BlockSpec((1, 1, d_ff, d), exp4), pl.BlockSpec((1, 1, 1, d), exp4),
                      pl.BlockSpec(sel.shape, lambda e, *_: (0, 0))],
            out_specs=any_spec,
            scratch_shapes=[pltpu.VMEM((d, d_ff2), BF16), pltpu.VMEM((d_ff, d), BF16),
                            pltpu.VMEM((2, tile8, LANES), F32), pltpu.VMEM((2, tile8, LANES), F32),
                            pltpu.VMEM((tile8, LANES), F32),
                            pltpu.SemaphoreType.DMA((2,)), pltpu.SemaphoreType.DMA((2,)),
                            pltpu.SemaphoreType.DMA]),
        out_shape=jax.ShapeDtypeStruct(xs.shape, F32),
        compiler_params=_params("arbitrary"),
        name="moe_expert",
    )(tiles_e, row_start, n_tiles, xs, w1, b1p, w2, b2, sel)


def _combine_body(tb, pos_ref, posn_ref, gates_ref, y_ref, ys_hbm, out_ref, buf, sem):
    i = pl.program_id(0)
    nsteps = pl.num_programs(0)
    slot = i % 2

    def issue(table, dst_slot):
        def one(t, carry):
            for k in range(TOP_K):
                pltpu.make_async_copy(_row_tile(ys_hbm, table[k, t]),
                                      _row_tile(buf.at[dst_slot, k], t),
                                      sem.at[dst_slot]).start(priority=k % DMA_THREADS)
            return carry
        lax.fori_loop(0, tb, one, 0, unroll=8)

    @pl.when(i == 0)
    def _():
        issue(pos_ref, 0)

    @pl.when(i + 1 < nsteps)
    def _():
        issue(posn_ref, 1 - slot)

    for k in range(TOP_K):
        pltpu.make_async_copy(ys_hbm.at[pl.ds(0, tb * SUBLANES)], buf.at[slot, k], sem.at[slot]).wait()
    gate = [jnp.broadcast_to(gates_ref[:, k:k + 1], (tb, LANES)) for k in range(TOP_K)]
    for s in range(SUBLANES):
        cols = slice(s * LANES, (s + 1) * LANES)
        acc = y_ref[:, cols]
        for k in range(TOP_K):
            acc = acc + gate[k] * buf[slot, k, pl.ds(s, tb, stride=SUBLANES), :]
        out_ref[:, cols] = acc


def _combine_call(pos, gates_t, y, ys):
    t, d = y.shape
    tb = COMBINE_BLOCK
    nb = t // tb
    smem = lambda fn: pl.BlockSpec((TOP_K, tb), fn, memory_space=pltpu.SMEM)
    return pl.pallas_call(
        functools.partial(_combine_body, tb),
        grid=(nb,),
        in_specs=[smem(lambda i: (0, i)), smem(lambda i: (0, jnp.minimum(i + 1, nb - 1))),
                  pl.BlockSpec((tb, TOP_K), lambda i: (i, 0)),
                  pl.BlockSpec((tb, d), lambda i: (i, 0)),
                  pl.BlockSpec(memory_space=pl.ANY)],
        out_specs=pl.BlockSpec((tb, d), lambda i: (i, 0)),
        out_shape=jax.ShapeDtypeStruct((t, d), F32),
        scratch_shapes=[pltpu.VMEM((2, TOP_K, tb * SUBLANES, LANES), F32),
                        pltpu.SemaphoreType.DMA((2,))],
        compiler_params=_params("arbitrary"),
        name="moe_combine",
    )(pos, pos, gates_t, y, ys)


def _moe(layer, y, g, w_r, b_r, w1, b1p, w2, b2, sel):
    t, d = y.shape
    n_exp = w_r.shape[1]
    tile = EXPERT_TILE
    max_tiles = (TOP_K * t) // tile + n_exp
    b_r_l = jnp.broadcast_to(b_r[:, None], (n_exp, LANES))
    ids, gates, rank, totals = _route_call(y, g, w_r.T, b_r_l)

    n_e = totals[:, 0]
    tiles_e = (n_e + tile - 1) // tile
    tile_end = jnp.cumsum(tiles_e)
    row_start = (tile_end - tiles_e) * tile
    n_tiles = tile_end[-1:]
    pos = rank
    for e in range(n_exp):
        pos = pos + jnp.where(ids == e, row_start[e], 0)
    spare = n_tiles[0] + jnp.arange(n_exp, dtype=jnp.int32)
    ztile = jnp.concatenate([jnp.where(n_e % tile != 0, (tile_end - 1) * tile, -1),
                             jnp.where(spare < max_tiles, spare * tile, -1)]).astype(jnp.int32)

    xs = _dispatch_call(pos, ztile, y, g, max_tiles * tile)
    ys = _expert_call(layer, tiles_e.astype(jnp.int32), row_start.astype(jnp.int32),
                      n_tiles.astype(jnp.int32), xs, w1, b1p, w2, b2, sel)
    return _combine_call(pos, gates.T, y, ys)


def _t5_bucket(dist):
    n = jnp.maximum(dist, 0)
    max_exact = NUM_BUCKETS // 2
    nf = jnp.maximum(n, max_exact).astype(F32)
    large = max_exact + (jnp.log(nf / max_exact) / math.log(MAX_DISTANCE / max_exact)
                         * (NUM_BUCKETS - max_exact)).astype(jnp.int32)
    large = jnp.minimum(large, NUM_BUCKETS - 1)
    return jnp.where(n < max_exact, n, large)


def _window_bias(dist, rel_bias, n_kv, q_per_kv):
    tq, tk = dist.shape
    hot = jax.nn.one_hot(_t5_bucket(dist), NUM_BUCKETS, dtype=F32)
    bias = jnp.einsum("qkb,bh->hqk", hot, rel_bias.astype(F32), precision=lax.Precision.HIGHEST)
    valid = (dist >= 0) & (dist < WINDOW)
    return jnp.where(valid, bias, NEG_INF).reshape(n_kv, q_per_kv * tq, tk)


def _selection_matrix():
    src = jnp.arange(2 * LANES)
    dst = (src % 2) * LANES + src // 2
    return jax.nn.one_hot(dst, 2 * LANES, dtype=BF16)


def kernel(x_prompt, x_sample, cache_k, cache_v, norm_mix_g, norm_ffn_g, a_w_in, a_b_in, a_ln_g,
           a_ln_b, a_w_s, a_b_s, a_w_out, a_b_out, s_w_qkv, s_b_qkv, s_q_norm_g, s_k_norm_g,
           s_sinks, s_w_o, s_b_o, rel_bias, m_w_router, m_b_router, m_w1, m_b1, m_w2, m_b2):
    batch, seq, d = x_prompt.shape
    dec_batch, dec_seq, _ = x_sample.shape
    depth = norm_mix_g.shape[0]
    n_groups = a_w_s.shape[1]
    n_heads = s_sinks.shape[1]
    n_kv, head_dim = cache_k.shape[3], cache_k.shape[4]
    q_per_kv = n_heads // n_kv
    cw = cache_k.shape[2]
    ns = dec_batch * dec_seq
    assert d == SUBLANES * LANES, "the routed rows are moved as one (8, 128) tile per token"
    assert 2 * head_dim == LANES and q_per_kv % 2 == 0, "two heads share one lane tile"
    assert ns % ROW_BLOCK == 0 and seq % ROW_BLOCK == 0 and CHUNK % dec_seq == 0
    assert (ns + batch * seq) % ROUTE_BLOCK == 0 and seq >= WINDOW and SUBLANES % dec_seq == 0

    y = jnp.concatenate([x_sample.reshape(ns, d), x_prompt.reshape(batch * seq, d)], axis=0)
    sel = _selection_matrix()
    ones_bd = jnp.kron(jnp.eye(LANES // head_dim, dtype=F32),
                       jnp.ones((head_dim, head_dim), F32)).astype(BF16)

    qi, kj = jnp.arange(WINDOW), jnp.arange(2 * WINDOW)
    bias_p = _window_bias(WINDOW + qi[:, None] - kj[None, :], rel_bias, n_kv, q_per_kv)
    by_parity = lambda a: (a.reshape(n_kv, q_per_kv // 2, 2, WINDOW, -1).swapaxes(1, 2)
                           .reshape(n_kv * q_per_kv * WINDOW, -1))
    bias_p = by_parity(bias_p)
    qt = jnp.arange(dec_seq)
    bias_c = _window_bias(qt[:, None] + cw - jnp.arange(cw)[None, :], rel_bias, n_kv, q_per_kv)
    tn = jnp.arange(SUBLANES)
    dist_n = jnp.where(tn[None, :] < dec_seq, qt[:, None] - tn[None, :], -1)
    bias_n = _window_bias(dist_n, rel_bias, n_kv, q_per_kv)

    n_exp, d_ff2 = m_b1.shape[1], m_b1.shape[2]
    b1p = (m_b1.reshape(depth, n_exp, d_ff2 // (2 * LANES), LANES, 2).swapaxes(3, 4)
           .reshape(depth, n_exp, 1, d_ff2))
    b2 = m_b2[:, :, None, :]

    row = lambda a: a.reshape(1, -1)
    new_kp, new_vp, new_ks, new_vs, new_va = [], [], [], [], []
    for i in range(depth):
        j = i // 2
        g_mix = row(norm_mix_g[i])
        if i % 2 == 0:
            w_s = jnp.tril(a_w_s[j])
            c = min(dec_seq, CHUNK)
            w_sample = jnp.einsum("ab,gpq->gapbq", jnp.eye(CHUNK // c, dtype=F32),
                                  jnp.tril(a_w_s[j][:, :c, :c])).reshape(n_groups, CHUNK, CHUNK)
            wsp = jnp.stack([w_sample, w_s[:, :CHUNK, :CHUNK]]).astype(BF16)
            dg = a_w_out.shape[1] // n_groups
            b_prompt = jnp.repeat(a_b_s[j][:, :CHUNK].T, dg, axis=1)
            b_sample = jnp.repeat(jnp.tile(a_b_s[j][:, :c].T, (CHUNK // c, 1)), dg, axis=1)
            bsp = jnp.stack([b_sample, b_prompt])
            y, v_s = _sgu_call(y, ns, g_mix, a_w_in[j].astype(BF16), row(a_b_in[j]), row(a_ln_g[j]),
                               row(a_ln_b[j]), wsp, bsp, a_w_out[j].astype(BF16), row(a_b_out[j]))
            new_va.append(v_s.reshape(dec_batch, dec_seq, -1))
        else:
            w_qkv, b_qkv = s_w_qkv[j].astype(BF16), row(s_b_qkv[j])
            qg = row(jnp.tile(s_q_norm_g[j], n_heads))
            kg = row(jnp.tile(s_k_norm_g[j], n_kv))
            w_o, b_o = s_w_o[j].astype(BF16), row(s_b_o[j])
            sinks = s_sinks[j].astype(F32).reshape(n_kv, q_per_kv, 1)
            sink_p = by_parity(jnp.repeat(sinks, WINDOW, axis=1))
            sink_s = jnp.repeat(sinks, dec_seq, axis=1)
            ck = cache_k[j].reshape(dec_batch, cw, n_kv * head_dim)
            cv = cache_v[j].reshape(dec_batch, cw, n_kv * head_dim)
            y, ks, vs = _swa_sample_call(y, ck, cv, dec_seq, g_mix, w_qkv, b_qkv, qg, kg, ones_bd,
                                         bias_c, bias_n, sink_s, w_o, b_o, n_kv, q_per_kv, head_dim)
            y, kp, vp = _swa_prompt_call(y, ns, batch, seq, g_mix, w_qkv, b_qkv, qg, kg, ones_bd,
                                         bias_p, sink_p, w_o, b_o, n_kv, q_per_kv, head_dim)
            new_ks.append(ks.reshape(dec_batch, cw, n_kv, head_dim))
            new_vs.append(vs.reshape(dec_batch, cw, n_kv, head_dim))
            new_kp.append(kp.reshape(batch, WINDOW, n_kv, head_dim))
            new_vp.append(vp.reshape(batch, WINDOW, n_kv, head_dim))
        y = _moe(i, y, row(norm_ffn_g[i]), m_w_router[i], m_b_router[i], m_w1, b1p, m_w2, b2, sel)
    yp = y[ns:].reshape(batch, seq, d)
    ys = y[:ns].reshape(dec_batch, dec_seq, d)
    return (yp, ys, jnp.stack(new_kp), jnp.stack(new_vp), jnp.stack(new_ks), jnp.stack(new_vs),
            jnp.stack(new_va))
```

```python
import functools
import math

import jax
import jax.numpy as jnp
import numpy as np
from jax import lax
from jax.experimental import pallas as pl
from jax.experimental.pallas import tpu as pltpu

EPS = 1e-6
NEG_INF = -1e30
CHUNK = 128
WINDOW = 128
NUM_BUCKETS = 32
MAX_DISTANCE = 128
TOP_K = 4
SWIGLU_LIMIT = 7.0
SWIGLU_ALPHA = 1.702

LANES = 128
SUBLANES = 8
VMEM_LIMIT_BYTES = 56 << 20
DMA_THREADS = 2

ROW_BLOCK = 256
ROUTE_BLOCK = 512
EXPERT_TILE = 256
DISPATCH_BLOCK = 512
COMBINE_BLOCK = 256
SAMPLE_SEQ_BLOCK = 32

F32 = jnp.float32
BF16 = jnp.bfloat16


def _params(*sem):
    return pltpu.CompilerParams(dimension_semantics=sem, vmem_limit_bytes=VMEM_LIMIT_BYTES)


def _rms(x, g):
    return x * lax.rsqrt(jnp.mean(x * x, axis=-1, keepdims=True) + EPS) * g


def _dot(a, b):
    return jnp.dot(a, b, preferred_element_type=F32)


def _dot_nt(a, b):
    return lax.dot_general(a, b, (((1,), (1,)), ((), ())), preferred_element_type=F32)


def _sgu_body(n_sample_blocks, n_groups, split_rows, *refs):
    i = pl.program_id(0)
    if split_rows:
        head_ref, tail_ref, *refs = refs
        x = jnp.where(i < n_sample_blocks, head_ref[...], tail_ref[...])
    else:
        y_ref, *refs = refs
        x = y_ref[...]
    (g_ref, win_ref, bin_ref, lng_ref, lnb_ref, wsp_ref, bsp_ref, wout_ref, bout_ref,
     y2_ref, v_ref) = refs
    xn = _rms(x, g_ref[...]).astype(BF16)
    z = _dot(xn, win_ref[...]) + bin_ref[...]
    z = 0.5 * z * (1.0 + lax.erf(z * (1.0 / math.sqrt(2.0))))
    half = z.shape[1] // 2
    u, v = z[:, :half], z[:, half:]
    mu = jnp.mean(v, axis=-1, keepdims=True)
    vc = v - mu
    vn = vc * lax.rsqrt(jnp.mean(vc * vc, axis=-1, keepdims=True) + EPS) * lng_ref[...] + lnb_ref[...]

    @pl.when(i < n_sample_blocks)
    def _():
        v_ref[...] = vn

    vb = vn.astype(BF16)
    dg = half // n_groups
    rows = []
    for c in range(x.shape[0] // CHUNK):
        r0 = c * CHUNK
        cols = []
        for g in range(n_groups):
            gate = _dot(wsp_ref[0, g], vb[r0:r0 + CHUNK, g * dg:(g + 1) * dg])
            gate = gate + bsp_ref[0, :, g * dg:(g + 1) * dg]
            cols.append(u[r0:r0 + CHUNK, g * dg:(g + 1) * dg] * gate)
        rows.append(jnp.concatenate(cols, axis=1))
    out = jnp.concatenate(rows, axis=0).astype(BF16)
    y2_ref[...] = x + _dot(out, wout_ref[...]) + bout_ref[...]


def _sgu_call(y, n_sample_rows, g, w_in, b_in, ln_g, ln_b, wsp, bsp, w_out, b_out):
    rb = ROW_BLOCK
    nsb = n_sample_rows // rb
    split = isinstance(y, tuple)
    if split:
        t, d = y[0].shape[0] + y[1].shape[0], y[0].shape[1]
        row_specs = [pl.BlockSpec((rb, d), lambda i: (jnp.minimum(i, nsb - 1), 0)),
                     pl.BlockSpec((rb, d), lambda i: (jnp.maximum(i - nsb, 0), 0))]
    else:
        t, d = y.shape
        y = (y,)
        row_specs = [pl.BlockSpec((rb, d), lambda i: (i, 0))]
    n_groups = wsp.shape[1]
    half = w_out.shape[0]
    const = lambda shape: pl.BlockSpec(shape, lambda i: (0,) * len(shape))
    which = lambda i: jnp.where(i < nsb, 0, 1)
    return pl.pallas_call(
        functools.partial(_sgu_body, nsb, n_groups, split),
        grid=(t // rb,),
        in_specs=row_specs + [
            const((1, d)), const(w_in.shape), const((1, 2 * half)), const((1, half)), const((1, half)),
            pl.BlockSpec((1, n_groups, CHUNK, CHUNK), lambda i: (which(i), 0, 0, 0)),
            pl.BlockSpec((1, CHUNK, half), lambda i: (which(i), 0, 0)),
            const(w_out.shape), const((1, d)),
        ],
        out_specs=[
            pl.BlockSpec((rb, d), lambda i: (i, 0)),
            pl.BlockSpec((rb, half), lambda i: (jnp.minimum(i, nsb - 1), 0)),
        ],
        out_shape=[jax.ShapeDtypeStruct((t, d), F32), jax.ShapeDtypeStruct((n_sample_rows, half), F32)],
        compiler_params=_params("arbitrary"),
        name="sgu",
    )(*y, g, w_in, b_in, ln_g, ln_b, wsp, bsp, w_out, b_out)


def _head_mean_square(z, ones_bd, head_dim):
    sq = z * z
    tiles = z.shape[1] // LANES
    st = jnp.concatenate([sq[:, c * LANES:(c + 1) * LANES] for c in range(tiles)], axis=0)
    hi = st.astype(BF16)
    lo = (st - hi.astype(F32)).astype(BF16)
    ms = (_dot(hi, ones_bd) + _dot(lo, ones_bd)) * (1.0 / head_dim)
    r = z.shape[0]
    return jnp.concatenate([ms[c * r:(c + 1) * r] for c in range(tiles)], axis=1)


def _qkv(x, g_ref, wqkv_ref, bqkv_ref, qg_ref, kg_ref, ones_ref, n_q, n_k, head_dim):
    xn = _rms(x, g_ref[...]).astype(BF16)
    qkv = _dot(xn, wqkv_ref[...]) + bqkv_ref[...]
    q, k, v = qkv[:, :n_q], qkv[:, n_q:n_q + n_k], qkv[:, n_q + n_k:]
    ones_bd = ones_ref[...]
    q = q * lax.rsqrt(_head_mean_square(q, ones_bd, head_dim) + EPS) * qg_ref[...]
    k = k * lax.rsqrt(_head_mean_square(k, ones_bd, head_dim) + EPS) * kg_ref[...]
    return q, k, v


def _swa_prompt_body(n_kv, q_per_kv, head_dim, y_ref, g_ref, wqkv_ref, bqkv_ref, qg_ref, kg_ref,
                     ones_ref, bias_ref, sink_ref, wo_ref, bo_ref, y2_ref, ko_ref, vo_ref,
                     kprev, vprev):
    n = pl.program_id(1)
    n_q, n_k = n_kv * q_per_kv * head_dim, n_kv * head_dim
    x = y_ref[...]
    q, k, v = _qkv(x, g_ref, wqkv_ref, bqkv_ref, qg_ref, kg_ref, ones_ref, n_q, n_k, head_dim)
    qb = (q * (head_dim ** -0.5)).astype(BF16)

    @pl.when(n == 0)
    def _():
        kprev[...] = jnp.zeros_like(kprev)
        vprev[...] = jnp.zeros_like(vprev)

    w = WINDOW
    low_half = lax.broadcasted_iota(jnp.int32, (w, LANES), 1) < head_dim

    def lo_hi(z, j):
        tile = z[:, (j // 2) * LANES:(j // 2 + 1) * LANES]
        same = jnp.where(low_half == (j % 2 == 0), tile, 0.0)
        other = pltpu.roll(same, head_dim, 1)
        lo, hi = (same, other) if j % 2 == 0 else (other, same)
        return lo.astype(BF16), hi.astype(BF16)

    pairs = q_per_kv // 2
    blk = pairs * w
    col = lax.broadcasted_iota(jnp.int32, (2 * n_kv * blk, 2 * w), 1)
    no_prev = jnp.logical_and(n == 0, col < w)
    sink = sink_ref[...]
    k_prev = [(kprev[j, 0], kprev[j, 1]) for j in range(n_kv)]
    v_prev = [(vprev[j, 0], vprev[j, 1]) for j in range(n_kv)]
    att_rows = []
    for c in range(x.shape[0] // w):
        rows = slice(c * w, (c + 1) * w)
        scores, values = [], []
        for j in range(n_kv):
            k_cur, v_cur = lo_hi(k[rows], j), lo_hi(v[rows], j)
            qj = jnp.concatenate([qb[rows, (j * pairs + a) * LANES:(j * pairs + a + 1) * LANES]
                                  for a in range(pairs)], axis=0)
            for half in range(2):
                scores.append(_dot_nt(qj, jnp.concatenate([k_prev[j][half], k_cur[half]], axis=0)))
            values.append(jnp.concatenate([v_prev[j][0], v_cur[0], v_prev[j][1], v_cur[1]], axis=0))
            k_prev[j], v_prev[j] = k_cur, v_cur
        bias = bias_ref[...]
        if c == 0:
            bias = jnp.where(no_prev, NEG_INF, bias)
        s = jnp.concatenate(scores, axis=0) + bias
        m = jnp.maximum(jnp.max(s, axis=1, keepdims=True), sink)
        p = jnp.exp(s - m)
        den = jnp.sum(p, axis=1, keepdims=True) + jnp.exp(sink - m)
        p = (p * (1.0 / den)).astype(BF16)
        tiles = []
        for j in range(n_kv):
            probs = jnp.concatenate([p[(2 * j) * blk:(2 * j + 1) * blk],
                                     p[(2 * j + 1) * blk:(2 * j + 2) * blk]], axis=1)
            o = _dot(probs, values[j])
            tiles.extend(o[a * w:(a + 1) * w] for a in range(pairs))
        att_rows.append(jnp.concatenate(tiles, axis=1))
    att = jnp.concatenate(att_rows, axis=0).astype(BF16)
    y2_ref[...] = x + _dot(att, wo_ref[...]) + bo_ref[...]
    for j in range(n_kv):
        for half in range(2):
            kprev[j, half] = k_prev[j][half]
            vprev[j, half] = v_prev[j][half]
    last = x.shape[0] - w

    @pl.when(n == pl.num_programs(1) - 1)
    def _():
        ko_ref[0] = k[last:]
        vo_ref[0] = v[last:]


def _swa_prompt_call(y, n_sample_rows, batch, seq, g, w_qkv, b_qkv, qg, kg, ones_bd, bias, sink,
                     w_o, b_o, n_kv, q_per_kv, head_dim):
    t, d = y.shape
    rb = ROW_BLOCK
    nb = seq // rb
    base = n_sample_rows // rb
    n_k = n_kv * head_dim
    const = lambda shape: pl.BlockSpec(shape, lambda b, n: (0,) * len(shape))
    row_spec = pl.BlockSpec((rb, d), lambda b, n: (base + b * nb + n, 0))
    kv_spec = pl.BlockSpec((1, WINDOW, n_k), lambda b, n: (b, 0, 0))
    return pl.pallas_call(
        functools.partial(_swa_prompt_body, n_kv, q_per_kv, head_dim),
        grid=(batch, nb),
        in_specs=[row_spec, const((1, d)), const(w_qkv.shape), const(b_qkv.shape), const(qg.shape),
                  const(kg.shape), const(ones_bd.shape), const(bias.shape), const(sink.shape),
                  const(w_o.shape), const((1, d))],
        out_specs=[row_spec, kv_spec, kv_spec],
        out_shape=[jax.ShapeDtypeStruct((t, d), F32),
                   jax.ShapeDtypeStruct((batch, WINDOW, n_k), F32),
                   jax.ShapeDtypeStruct((batch, WINDOW, n_k), F32)],
        scratch_shapes=[pltpu.VMEM((n_kv, 2, WINDOW, LANES), BF16),
                        pltpu.VMEM((n_kv, 2, WINDOW, LANES), BF16)],
        input_output_aliases={0: 0},
        compiler_params=_params("arbitrary", "arbitrary"),
        name="swa_prompt",
    )(y, g, w_qkv, b_qkv, qg, kg, ones_bd, bias, sink, w_o, b_o)


def _swa_sample_body(n_kv, q_per_kv, head_dim, dec_seq, y_ref, ck_ref, cv_ref, g_ref, wqkv_ref,
                     bqkv_ref, qg_ref, kg_ref, ones_ref, biasc_ref, biasn_ref, sink_ref, wo_ref,
                     bo_ref, y2_ref, ko_ref, vo_ref, q_scr, k_scr, v_scr, att_scr):
    n_q, n_k = n_kv * q_per_kv * head_dim, n_kv * head_dim
    x = y_ref[...]
    q, k, v = _qkv(x, g_ref, wqkv_ref, bqkv_ref, qg_ref, kg_ref, ones_ref, n_q, n_k, head_dim)
    q_scr[...] = q * (head_dim ** -0.5)
    k_scr[...] = k
    v_scr[...] = v
    cw = ck_ref.shape[1]
    pad = jnp.zeros((SUBLANES - dec_seq, n_k), F32)

    per_group = SUBLANES // dec_seq
    blk = q_per_kv * dec_seq

    def per_group_of_rows(p, carry):
        r = pl.multiple_of(p * SUBLANES, SUBLANES)
        q8, k8, v8 = q_scr[pl.ds(r, SUBLANES), :], k_scr[pl.ds(r, SUBLANES), :], v_scr[pl.ds(r, SUBLANES), :]
        s_cache, s_new, values = [], [], []
        for u in range(per_group):
            s = p * per_group + u
            rows = slice(u * dec_seq, (u + 1) * dec_seq)
            ck, cv = ck_ref[s], cv_ref[s]
            k_new, v_new = k8[rows], v8[rows]
            ko_ref[s, 0:cw - dec_seq, :] = ck[dec_seq:]
            ko_ref[s, cw - dec_seq:cw, :] = k_new
            vo_ref[s, 0:cw - dec_seq, :] = cv[dec_seq:]
            vo_ref[s, cw - dec_seq:cw, :] = v_new
            qs = q8[rows].astype(BF16)
            ckb, cvb = ck.astype(BF16), cv.astype(BF16)
            kn8 = jnp.concatenate([k_new, pad], axis=0).astype(BF16)
            vn8 = jnp.concatenate([v_new, pad], axis=0).astype(BF16)
            for j in range(n_kv):
                lo, hi = j * head_dim, (j + 1) * head_dim
                qj = jnp.concatenate(
                    [qs[:, (j * q_per_kv + g) * head_dim:(j * q_per_kv + g + 1) * head_dim]
                     for g in range(q_per_kv)], axis=0)
                s_cache.append(_dot_nt(qj, ckb[:, lo:hi]))
                s_new.append(_dot_nt(qj, kn8[:, lo:hi]))
                values.append((cvb[:, lo:hi], vn8[:, lo:hi]))
        s_c = jnp.concatenate(s_cache, axis=0) + biasc_ref[...]
        s_n = jnp.concatenate(s_new, axis=0) + biasn_ref[...]
        sink = sink_ref[...]
        m = jnp.maximum(jnp.maximum(jnp.max(s_c, axis=1, keepdims=True),
                                    jnp.max(s_n, axis=1, keepdims=True)), sink)
        p_c, p_n = jnp.exp(s_c - m), jnp.exp(s_n - m)
        inv = 1.0 / (jnp.sum(p_c, axis=1, keepdims=True) + jnp.sum(p_n, axis=1, keepdims=True)
                     + jnp.exp(sink - m))
        p_c, p_n = (p_c * inv).astype(BF16), (p_n * inv).astype(BF16)
        seq_rows = []
        for u in range(per_group):
            heads = []
            for j in range(n_kv):
                b = u * n_kv + j
                o = (_dot(p_c[b * blk:(b + 1) * blk], values[b][0])
                     + _dot(p_n[b * blk:(b + 1) * blk], values[b][1]))
                heads.extend(o[g * dec_seq:(g + 1) * dec_seq] for g in range(q_per_kv))
            seq_rows.append(jnp.concatenate(heads, axis=1))
        att_scr[pl.ds(r, SUBLANES), :] = jnp.concatenate(seq_rows, axis=0)
        return carry

    lax.fori_loop(0, ck_ref.shape[0] // per_group, per_group_of_rows, 0)
    y2_ref[...] = x + _dot(att_scr[...].astype(BF16), wo_ref[...]) + bo_ref[...]


def _swa_sample_call(y, ck, cv, dec_seq, g, w_qkv, b_qkv, qg, kg, ones_bd, bias_c, bias_n, sink,
                     w_o, b_o, n_kv, q_per_kv, head_dim):
    t, d = y.shape
    dec_batch, cw, n_k = ck.shape
    sb = min(SAMPLE_SEQ_BLOCK, dec_batch)
    rb = sb * dec_seq
    n_q = n_kv * q_per_kv * head_dim
    const = lambda shape: pl.BlockSpec(shape, lambda i: (0,) * len(shape))
    row_spec = pl.BlockSpec((rb, d), lambda i: (i, 0))
    cache_spec = pl.BlockSpec((sb, cw, n_k), lambda i: (i, 0, 0))
    return pl.pallas_call(
        functools.partial(_swa_sample_body, n_kv, q_per_kv, head_dim, dec_seq),
        grid=(dec_batch // sb,),
        in_specs=[row_spec, cache_spec, cache_spec, const((1, d)), const(w_qkv.shape),
                  const(b_qkv.shape), const(qg.shape), const(kg.shape), const(ones_bd.shape),
                  const(bias_c.shape), const(bias_n.shape), const(sink.shape), const(w_o.shape),
                  const((1, d))],
        out_specs=[row_spec, cache_spec, cache_spec],
        out_shape=[jax.ShapeDtypeStruct((t, d), F32),
                   jax.ShapeDtypeStruct(ck.shape, F32), jax.ShapeDtypeStruct(cv.shape, F32)],
        scratch_shapes=[pltpu.VMEM((rb, n_q), F32), pltpu.VMEM((rb, n_k), F32),
                        pltpu.VMEM((rb, n_k), F32), pltpu.VMEM((rb, n_q), F32)],
        input_output_aliases={0: 0},
        compiler_params=_params("arbitrary"),
        name="swa_sample",
    )(y, ck, cv, g, w_qkv, b_qkv, qg, kg, ones_bd, bias_c, bias_n, sink, w_o, b_o)


def _route_body(y_ref, g_ref, wr_ref, br_ref, ids_ref, gates_ref, rank_ref, tot_ref, carry):
    i = pl.program_id(0)

    @pl.when(i == 0)
    def _():
        carry[...] = jnp.zeros_like(carry)

    xn = _rms(y_ref[...], g_ref[...])
    logits = lax.dot_general(wr_ref[...], xn, (((1,), (1,)), ((), ())),
                             precision=lax.Precision.HIGHEST, preferred_element_type=F32)
    logits = logits + br_ref[:, 0:1]
    n_exp, tb = logits.shape
    e_iota = lax.broadcasted_iota(jnp.int32, (n_exp, tb), 0)
    work = logits
    vals, ids, hots = [], [], []
    for _ in range(TOP_K):
        m = jnp.max(work, axis=0, keepdims=True)
        idx = jnp.min(jnp.where(work == m, e_iota, n_exp), axis=0, keepdims=True)
        hot = e_iota == idx
        vals.append(m)
        ids.append(idx)
        hots.append(hot)
        work = jnp.where(hot, -jnp.inf, work)
    ex = [jnp.exp(v - vals[0]) for v in vals]
    den = ex[0] + ex[1] + ex[2] + ex[3]
    ids_ref[...] = jnp.concatenate(ids, axis=0)
    gates_ref[...] = jnp.concatenate([e / den for e in ex], axis=0)

    hot_all = jnp.zeros((n_exp, tb), F32)
    for hot in hots:
        hot_all = hot_all + hot.astype(F32)
    before = (lax.broadcasted_iota(jnp.int32, (tb, tb), 0)
              < lax.broadcasted_iota(jnp.int32, (tb, tb), 1)).astype(BF16)
    count = _dot(hot_all.astype(BF16), before) + carry[:, 0:1]
    rank_ref[...] = jnp.concatenate(
        [jnp.sum(jnp.where(hot, count, 0.0), axis=0, keepdims=True) for hot in hots],
        axis=0).astype(jnp.int32)
    total = carry[...] + jnp.sum(hot_all, axis=1, keepdims=True)
    carry[...] = total
    tot_ref[...] = total.astype(jnp.int32)


def _route_call(y, g, w_r_t, b_r):
    t, d = y.shape
    n_exp = w_r_t.shape[0]
    tb = ROUTE_BLOCK
    const = lambda shape: pl.BlockSpec(shape, lambda i: (0,) * len(shape))
    tok = lambda dtype: jax.ShapeDtypeStruct((TOP_K, t), dtype)
    tok_spec = pl.BlockSpec((TOP_K, tb), lambda i: (0, i))
    return pl.pallas_call(
        _route_body,
        grid=(t // tb,),
        in_specs=[pl.BlockSpec((tb, d), lambda i: (i, 0)), const((1, d)), const((n_exp, d)),
                  const((n_exp, LANES))],
        out_specs=[tok_spec, tok_spec, tok_spec, const((n_exp, LANES))],
        out_shape=[tok(jnp.int32), tok(F32), tok(jnp.int32),
                   jax.ShapeDtypeStruct((n_exp, LANES), jnp.int32)],
        scratch_shapes=[pltpu.VMEM((n_exp, LANES), F32)],
        compiler_params=_params("arbitrary"),
        name="moe_route",
    )(y, g, w_r_t, b_r)


def _store_row_tiles(ref, x):
    n = x.shape[0]
    for s in range(SUBLANES):
        ref[pl.ds(s, n, stride=SUBLANES), :] = x[:, s * LANES:(s + 1) * LANES]


def _load_row_tiles(ref, n):
    return jnp.concatenate([ref[pl.ds(s, n, stride=SUBLANES), :] for s in range(SUBLANES)], axis=1)


def _row_tile(ref, row):
    return ref.at[pl.ds(pl.multiple_of(row * SUBLANES, SUBLANES), SUBLANES)]


def _dispatch_body(tb, tile, ztile_ref, pos_ref, y_ref, g_ref, xs_hbm, xn_scr, zeros, sem, zsem):
    i = pl.program_id(0)
    tile8 = tile * SUBLANES

    @pl.when(i == 0)
    def _():
        zeros[...] = jnp.zeros_like(zeros)

        def fill(e):
            start = pl.multiple_of(ztile_ref[e] * SUBLANES, tile8)
            return pltpu.make_async_copy(zeros, xs_hbm.at[pl.ds(start, tile8)], zsem)

        for e in range(ztile_ref.shape[0]):
            @pl.when(ztile_ref[e] >= 0)
            def _():
                fill(e).start()
        for e in range(ztile_ref.shape[0]):
            @pl.when(ztile_ref[e] >= 0)
            def _():
                fill(e).wait()

    slot = i % 2
    rows = xn_scr.at[slot]

    def drain(s):
        for k in range(TOP_K):
            pltpu.make_async_copy(xn_scr.at[s], xs_hbm.at[pl.ds(0, tb * SUBLANES)], sem.at[s]).wait()

    @pl.when(i >= 2)
    def _():
        drain(slot)

    _store_row_tiles(rows, _rms(y_ref[...], g_ref[...]))

    def issue(t, carry):
        for k in range(TOP_K):
            pltpu.make_async_copy(_row_tile(rows, t), _row_tile(xs_hbm, pos_ref[k, t]),
                                  sem.at[slot]).start(priority=k % DMA_THREADS)
        return carry

    lax.fori_loop(0, tb, issue, 0, unroll=8)

    @pl.when(i == pl.num_programs(0) - 1)
    def _():
        @pl.when(i >= 1)
        def _():
            drain(1 - slot)

        drain(slot)


def _dispatch_call(pos, ztile, y, g, n_rows_padded):
    t, d = y.shape
    tb = DISPATCH_BLOCK
    tile = EXPERT_TILE
    return pl.pallas_call(
        functools.partial(_dispatch_body, tb, tile),
        grid_spec=pltpu.PrefetchScalarGridSpec(
            num_scalar_prefetch=1,
            grid=(t // tb,),
            in_specs=[pl.BlockSpec((TOP_K, tb), lambda i, z: (0, i), memory_space=pltpu.SMEM),
                      pl.BlockSpec((tb, d), lambda i, z: (i, 0)),
                      pl.BlockSpec((1, d), lambda i, z: (0, 0))],
            out_specs=pl.BlockSpec(memory_space=pl.ANY),
            scratch_shapes=[pltpu.VMEM((2, tb * SUBLANES, LANES), F32),
                            pltpu.VMEM((tile * SUBLANES, LANES), F32),
                            pltpu.SemaphoreType.DMA((2,)), pltpu.SemaphoreType.DMA]),
        out_shape=jax.ShapeDtypeStruct((n_rows_padded * SUBLANES, LANES), F32),
        compiler_params=_params("arbitrary"),
        name="moe_dispatch",
    )(ztile, pos, y, g)


def _expert_body(tile, max_tiles, tiles_ref, start_ref, used_ref, xs_hbm, w1_ref, b1_ref, w2_ref,
                 b2_ref, sel_ref, ys_hbm, w1p, w2b, xbuf, obuf, zeros, sem_in, sem_out, zsem):
    e = pl.program_id(0)
    nt = tiles_ref[e]
    d_ff2 = w1p.shape[1]
    pair = 2 * LANES
    tile8 = tile * SUBLANES

    def rows_of(tile_index):
        start = pl.multiple_of((start_ref[e] + tile_index * tile) * SUBLANES, tile8)
        return pl.ds(start, tile8)

    def in_copy(t, slot):
        return pltpu.make_async_copy(xs_hbm.at[rows_of(t)], xbuf.at[slot, pl.ds((t % 2) * tile8, tile8)],
                                     sem_in.at[slot])

    def out_copy(t, slot):
        return pltpu.make_async_copy(obuf.at[slot, pl.ds((t % 2) * tile8, tile8)], ys_hbm.at[rows_of(t)],
                                     sem_out.at[slot])

    tile_thread = DMA_THREADS - 1

    def for_item_tiles(item, slot, action):
        action(2 * item, slot)

        @pl.when(2 * item + 1 < nt)
        def _():
            action(2 * item + 1, slot)

    def compute(slot, n_rows):
        x = _load_row_tiles(xbuf.at[slot, pl.ds(0, n_rows * SUBLANES)], n_rows).astype(BF16)
        h = _dot(x, w1p[...]) + b1_ref[0, 0]
        acts = []
        for c in range(d_ff2 // pair):
            gate = jnp.minimum(h[:, c * pair:c * pair + LANES], SWIGLU_LIMIT)
            up = jnp.clip(h[:, c * pair + LANES:(c + 1) * pair], -SWIGLU_LIMIT, SWIGLU_LIMIT)
            acts.append((up + 1.0) * (gate * (1.0 / (1.0 + jnp.exp(-SWIGLU_ALPHA * gate)))))
        a = jnp.concatenate(acts, axis=1).astype(BF16)
        _store_row_tiles(obuf.at[slot, pl.ds(0, n_rows * SUBLANES)], _dot(a, w2b[...]) + b2_ref[0, 0])

    def run_item(item, n_rows):
        slot = item % 2

        @pl.when(2 * item + 2 < nt)
        def _():
            for_item_tiles(item + 1, 1 - slot, lambda t, s: in_copy(t, s).start(priority=tile_thread))

        for_item_tiles(item, slot, lambda t, s: in_copy(t, s).wait())
        compute(slot, n_rows)
        for_item_tiles(item, slot, lambda t, s: out_copy(t, s).start(priority=tile_thread))

        @pl.when(item >= 1)
        def _():
            for_item_tiles(item - 1, 1 - slot, lambda t, s: out_copy(t, s).wait())

    @pl.when(nt > 0)
    def _():
        for_item_tiles(0, 0, lambda t, s: in_copy(t, s).start(priority=tile_thread))
        sel = sel_ref[...]
        for c in range(d_ff2 // pair):
            blk = w1_ref[0, 0, :, c * pair:(c + 1) * pair].astype(BF16)
            w1p[:, c * pair:(c + 1) * pair] = _dot(blk, sel).astype(BF16)
        w2b[...] = w2_ref[0, 0].astype(BF16)

        n_full = nt // 2

        def step(item, carry):
            run_item(item, 2 * tile)
            return carry

        lax.fori_loop(0, n_full, step, 0)

        @pl.when(nt % 2 == 1)
        def _():
            run_item(n_full, tile)

        last = (nt - 1) // 2
        for_item_tiles(last, last % 2, lambda t, s: out_copy(t, s).wait())

    @pl.when(e == pl.num_programs(0) - 1)
    def _():
        zeros[...] = jnp.zeros_like(zeros)
        spare = max_tiles - used_ref[0]

        def fill(s):
            start = pl.multiple_of((used_ref[0] + s) * tile8, tile8)
            return pltpu.make_async_copy(zeros, ys_hbm.at[pl.ds(start, tile8)], zsem)

        def start_fill(s, carry):
            fill(s).start()
            return carry

        def wait_fill(s, carry):
            fill(s).wait()
            return carry

        lax.fori_loop(0, spare, start_fill, 0)
        lax.fori_loop(0, spare, wait_fill, 0)


def _expert_call(layer, tiles_e, row_start, n_tiles, xs, w1, b1p, w2, b2, sel):
    tile = EXPERT_TILE
    tile8 = tile * SUBLANES
    max_tiles = xs.shape[0] // tile8
    n_exp, d, d_ff2 = w1.shape[1], w1.shape[2], w1.shape[3]
    d_ff = w2.shape[2]
    exp4 = lambda e, *_: (layer, e, 0, 0)
    any_spec = pl.BlockSpec(memory_space=pl.ANY)
    return pl.pallas_call(
        functools.partial(_expert_body, tile, max_tiles),
        grid_spec=pltpu.PrefetchScalarGridSpec(
            num_scalar_prefetch=3,
            grid=(n_exp,),
            in_specs=[any_spec,
                      pl.BlockSpec((1, 1, d, d_ff2), exp4), pl.BlockSpec((1, 1, 1, d_ff2), exp4),
                      pl.BlockSpec((1, 1, d_ff, d), exp4), pl.BlockSpec((1, 1, 1, d), exp4),
                      pl.BlockSpec(sel.shape, lambda e, *_: (0, 0))],
            out_specs=any_spec,
            scratch_shapes=[pltpu.VMEM((d, d_ff2), BF16), pltpu.VMEM((d_ff, d), BF16),
                            pltpu.VMEM((2, 2 * tile8, LANES), F32),
                            pltpu.VMEM((2, 2 * tile8, LANES), F32),
                            pltpu.VMEM((tile8, LANES), F32),
                            pltpu.SemaphoreType.DMA((2,)), pltpu.SemaphoreType.DMA((2,)),
                            pltpu.SemaphoreType.DMA]),
        out_shape=jax.ShapeDtypeStruct(xs.shape, F32),
        compiler_params=_params("arbitrary"),
        name="moe_expert",
    )(tiles_e, row_start, n_tiles, xs, w1, b1p, w2, b2, sel)


def _combine_body(tb, n_head_blocks, pos_ref, posn_ref, gates_ref, y_ref, ys_hbm, *refs):
    *out_refs, buf, sem = refs
    i = pl.program_id(0)
    nsteps = pl.num_programs(0)
    slot = i % 2

    def issue(table, dst_slot):
        def one(t, carry):
            for k in range(TOP_K):
                pltpu.make_async_copy(_row_tile(ys_hbm, table[k, t]),
                                      _row_tile(buf.at[dst_slot, k], t),
                                      sem.at[dst_slot]).start(priority=k % DMA_THREADS)
            return carry
        lax.fori_loop(0, tb, one, 0, unroll=8)

    @pl.when(i == 0)
    def _():
        issue(pos_ref, 0)

    @pl.when(i + 1 < nsteps)
    def _():
        issue(posn_ref, 1 - slot)

    for k in range(TOP_K):
        pltpu.make_async_copy(ys_hbm.at[pl.ds(0, tb * SUBLANES)], buf.at[slot, k], sem.at[slot]).wait()
    gate = [jnp.broadcast_to(gates_ref[:, k:k + 1], (tb, LANES)) for k in range(TOP_K)]
    accs = []
    for s in range(SUBLANES):
        acc = y_ref[:, s * LANES:(s + 1) * LANES]
        for k in range(TOP_K):
            acc = acc + gate[k] * buf[slot, k, pl.ds(s, tb, stride=SUBLANES), :]
        accs.append(acc)

    def write(ref):
        for s in range(SUBLANES):
            ref[:, s * LANES:(s + 1) * LANES] = accs[s]

    if n_head_blocks is None:
        write(out_refs[0])
    else:
        pl.when(i < n_head_blocks)(lambda: write(out_refs[0]))
        pl.when(i >= n_head_blocks)(lambda: write(out_refs[1]))


def _combine_call(pos, gates_t, y, ys, n_head_rows=None):
    t, d = y.shape
    tb = COMBINE_BLOCK
    nb = t // tb
    smem = lambda fn: pl.BlockSpec((TOP_K, tb), fn, memory_space=pltpu.SMEM)
    if n_head_rows is None:
        nhb = None
        out_specs = pl.BlockSpec((tb, d), lambda i: (i, 0))
        out_shape = jax.ShapeDtypeStruct((t, d), F32)
    else:
        nhb = n_head_rows // tb
        out_specs = [pl.BlockSpec((tb, d), lambda i: (jnp.minimum(i, nhb - 1), 0)),
                     pl.BlockSpec((tb, d), lambda i: (jnp.maximum(i - nhb, 0), 0))]
        out_shape = [jax.ShapeDtypeStruct((n_head_rows, d), F32),
                     jax.ShapeDtypeStruct((t - n_head_rows, d), F32)]
    return pl.pallas_call(
        functools.partial(_combine_body, tb, nhb),
        grid=(nb,),
        in_specs=[smem(lambda i: (0, i)), smem(lambda i: (0, jnp.minimum(i + 1, nb - 1))),
                  pl.BlockSpec((tb, TOP_K), lambda i: (i, 0)),
                  pl.BlockSpec((tb, d), lambda i: (i, 0)),
                  pl.BlockSpec(memory_space=pl.ANY)],
        out_specs=out_specs,
        out_shape=out_shape,
        scratch_shapes=[pltpu.VMEM((2, TOP_K, tb * SUBLANES, LANES), F32),
                        pltpu.SemaphoreType.DMA((2,))],
        compiler_params=_params("arbitrary"),
        name="moe_combine",
    )(pos, pos, gates_t, y, ys)


def _moe(layer, y, g, w_r, b_r, w1, b1p, w2, b2, sel, n_head_rows=None):
    t, d = y.shape
    n_exp = w_r.shape[1]
    tile = EXPERT_TILE
    max_tiles = (TOP_K * t) // tile + n_exp
    b_r_l = jnp.broadcast_to(b_r[:, None], (n_exp, LANES))
    ids, gates, rank, totals = _route_call(y, g, w_r.T, b_r_l)

    n_e = totals[:, 0]
    tiles_e = (n_e + tile - 1) // tile
    tile_end = jnp.cumsum(tiles_e)
    row_start = (tile_end - tiles_e) * tile
    n_tiles = tile_end[-1:]
    pos = rank
    for e in range(n_exp):
        pos = pos + jnp.where(ids == e, row_start[e], 0)
    spare = n_tiles[0] + jnp.arange(n_exp, dtype=jnp.int32)
    ztile = jnp.concatenate([jnp.where(n_e % tile != 0, (tile_end - 1) * tile, -1),
                             jnp.where(spare < max_tiles, spare * tile, -1)]).astype(jnp.int32)

    xs = _dispatch_call(pos, ztile, y, g, max_tiles * tile)
    ys = _expert_call(layer, tiles_e.astype(jnp.int32), row_start.astype(jnp.int32),
                      n_tiles.astype(jnp.int32), xs, w1, b1p, w2, b2, sel)
    return _combine_call(pos, gates.T, y, ys, n_head_rows)


def _t5_bucket(dist):
    n = np.maximum(dist, 0)
    max_exact = NUM_BUCKETS // 2
    nf = np.maximum(n, max_exact).astype(np.float32)
    large = max_exact + (np.log(nf / np.float32(max_exact))
                         / np.float32(math.log(MAX_DISTANCE / max_exact))
                         * np.float32(NUM_BUCKETS - max_exact)).astype(np.int32)
    large = np.minimum(large, NUM_BUCKETS - 1)
    return np.where(n < max_exact, n, large)


def _window_bias(dist, rel_bias, n_kv, q_per_kv):
    tq, tk = dist.shape
    hot = (_t5_bucket(dist)[..., None] == np.arange(NUM_BUCKETS)).astype(np.float32)
    bias = jnp.einsum("qkb,bh->hqk", hot, rel_bias.astype(F32), precision=lax.Precision.HIGHEST)
    valid = (dist >= 0) & (dist < WINDOW)
    return jnp.where(valid, bias, NEG_INF).reshape(n_kv, q_per_kv * tq, tk)


def _selection_matrix():
    src = jnp.arange(2 * LANES)
    dst = (src % 2) * LANES + src // 2
    return jax.nn.one_hot(dst, 2 * LANES, dtype=BF16)


def kernel(x_prompt, x_sample, cache_k, cache_v, norm_mix_g, norm_ffn_g, a_w_in, a_b_in, a_ln_g,
           a_ln_b, a_w_s, a_b_s, a_w_out, a_b_out, s_w_qkv, s_b_qkv, s_q_norm_g, s_k_norm_g,
           s_sinks, s_w_o, s_b_o, rel_bias, m_w_router, m_b_router, m_w1, m_b1, m_w2, m_b2):
    batch, seq, d = x_prompt.shape
    dec_batch, dec_seq, _ = x_sample.shape
    depth = norm_mix_g.shape[0]
    n_groups = a_w_s.shape[1]
    n_heads = s_sinks.shape[1]
    n_kv, head_dim = cache_k.shape[3], cache_k.shape[4]
    q_per_kv = n_heads // n_kv
    cw = cache_k.shape[2]
    ns = dec_batch * dec_seq
    assert d == SUBLANES * LANES, "the routed rows are moved as one (8, 128) tile per token"
    assert 2 * head_dim == LANES and q_per_kv % 2 == 0, "two heads share one lane tile"
    assert ns % ROW_BLOCK == 0 and seq % ROW_BLOCK == 0 and CHUNK % dec_seq == 0
    assert ns % COMBINE_BLOCK == 0 and depth >= 1
    assert (ns + batch * seq) % ROUTE_BLOCK == 0 and seq >= WINDOW and SUBLANES % dec_seq == 0

    y = (x_sample.reshape(ns, d), x_prompt.reshape(batch * seq, d))
    sel = _selection_matrix()
    ones_bd = jnp.kron(jnp.eye(LANES // head_dim, dtype=F32),
                       jnp.ones((head_dim, head_dim), F32)).astype(BF16)

    qi, kj = np.arange(WINDOW), np.arange(2 * WINDOW)
    bias_p = _window_bias(WINDOW + qi[:, None] - kj[None, :], rel_bias, n_kv, q_per_kv)
    by_parity = lambda a: (a.reshape(n_kv, q_per_kv // 2, 2, WINDOW, -1).swapaxes(1, 2)
                           .reshape(n_kv * q_per_kv * WINDOW, -1))
    bias_p = by_parity(bias_p)
    qt = np.arange(dec_seq)
    bias_c = _window_bias(qt[:, None] + cw - np.arange(cw)[None, :], rel_bias, n_kv, q_per_kv)
    tn = np.arange(SUBLANES)
    dist_n = np.where(tn[None, :] < dec_seq, qt[:, None] - tn[None, :], -1)
    bias_n = _window_bias(dist_n, rel_bias, n_kv, q_per_kv)
    per_group = lambda a: jnp.tile(a.reshape(n_kv * q_per_kv * dec_seq, -1), (SUBLANES // dec_seq, 1))
    bias_c, bias_n = per_group(bias_c), per_group(bias_n)

    n_exp, d_ff2 = m_b1.shape[1], m_b1.shape[2]
    b1p = (m_b1.reshape(depth, n_exp, d_ff2 // (2 * LANES), LANES, 2).swapaxes(3, 4)
           .reshape(depth, n_exp, 1, d_ff2))
    b2 = m_b2[:, :, None, :]

    row = lambda a: a.reshape(1, -1)
    new_kp, new_vp, new_ks, new_vs, new_va = [], [], [], [], []
    for i in range(depth):
        j = i // 2
        g_mix = row(norm_mix_g[i])
        if i % 2 == 0:
            w_s = jnp.tril(a_w_s[j])
            c = min(dec_seq, CHUNK)
            w_sample = jnp.einsum("ab,gpq->gapbq", jnp.eye(CHUNK // c, dtype=F32),
                                  jnp.tril(a_w_s[j][:, :c, :c])).reshape(n_groups, CHUNK, CHUNK)
            wsp = jnp.stack([w_sample, w_s[:, :CHUNK, :CHUNK]]).astype(BF16)
            dg = a_w_out.shape[1] // n_groups
            b_prompt = jnp.repeat(a_b_s[j][:, :CHUNK].T, dg, axis=1)
            b_sample = jnp.repeat(jnp.tile(a_b_s[j][:, :c].T, (CHUNK // c, 1)), dg, axis=1)
            bsp = jnp.stack([b_sample, b_prompt])
            y, v_s = _sgu_call(y, ns, g_mix, a_w_in[j].astype(BF16), row(a_b_in[j]), row(a_ln_g[j]),
                               row(a_ln_b[j]), wsp, bsp, a_w_out[j].astype(BF16), row(a_b_out[j]))
            new_va.append(v_s.reshape(dec_batch, dec_seq, -1))
        else:
            w_qkv, b_qkv = s_w_qkv[j].astype(BF16), row(s_b_qkv[j])
            qg = row(jnp.tile(s_q_norm_g[j], n_heads))
            kg = row(jnp.tile(s_k_norm_g[j], n_kv))
            w_o, b_o = s_w_o[j].astype(BF16), row(s_b_o[j])
            sinks = s_sinks[j].astype(F32).reshape(n_kv, q_per_kv, 1)
            sink_p = by_parity(jnp.repeat(sinks, WINDOW, axis=1))
            sink_s = per_group(jnp.repeat(sinks, dec_seq, axis=1))
            ck = cache_k[j].reshape(dec_batch, cw, n_kv * head_dim)
            cv = cache_v[j].reshape(dec_batch, cw, n_kv * head_dim)
            y, ks, vs = _swa_sample_call(y, ck, cv, dec_seq, g_mix, w_qkv, b_qkv, qg, kg, ones_bd,
                                         bias_c, bias_n, sink_s, w_o, b_o, n_kv, q_per_kv, head_dim)
            y, kp, vp = _swa_prompt_call(y, ns, batch, seq, g_mix, w_qkv, b_qkv, qg, kg, ones_bd,
                                         bias_p, sink_p, w_o, b_o, n_kv, q_per_kv, head_dim)
            new_ks.append(ks.reshape(dec_batch, cw, n_kv, head_dim))
            new_vs.append(vs.reshape(dec_batch, cw, n_kv, head_dim))
            new_kp.append(kp.reshape(batch, WINDOW, n_kv, head_dim))
            new_vp.append(vp.reshape(batch, WINDOW, n_kv, head_dim))
        y = _moe(i, y, row(norm_ffn_g[i]), m_w_router[i], m_b_router[i], m_w1, b1p, m_w2, b2, sel,
                 n_head_rows=ns if i == depth - 1 else None)
    ys, yp = y
    yp = yp.reshape(batch, seq, d)
    ys = ys.reshape(dec_batch, dec_seq, d)
    return (yp, ys, jnp.stack(new_kp), jnp.stack(new_vp), jnp.stack(new_ks), jnp.stack(new_vs),
            jnp.stack(new_va))
```

```python
import functools
import math

import jax
import jax.numpy as jnp
import numpy as np
from jax import lax
from jax.experimental import pallas as pl
from jax.experimental.pallas import tpu as pltpu

EPS = 1e-6
NEG_INF = -1e30
CHUNK = 128
WINDOW = 128
NUM_BUCKETS = 32
MAX_DISTANCE = 128
TOP_K = 4
SWIGLU_LIMIT = 7.0
SWIGLU_ALPHA = 1.702

LANES = 128
SUBLANES = 8
VMEM_LIMIT_BYTES = 56 << 20
DMA_THREADS = 2

ROW_BLOCK = 256
ROUTE_BLOCK = 512
EXPERT_TILE = 256
DISPATCH_BLOCK = 512
COMBINE_BLOCK = 256
SAMPLE_SEQ_BLOCK = 32

F32 = jnp.float32
BF16 = jnp.bfloat16


def _params(*sem):
    return pltpu.CompilerParams(dimension_semantics=sem, vmem_limit_bytes=VMEM_LIMIT_BYTES)


def _rms(x, g):
    return x * lax.rsqrt(jnp.mean(x * x, axis=-1, keepdims=True) + EPS) * g


def _dot(a, b):
    return jnp.dot(a, b, preferred_element_type=F32)


def _dot_nt(a, b):
    return lax.dot_general(a, b, (((1,), (1,)), ((), ())), preferred_element_type=F32)


def _sgu_body(n_sample_blocks, n_groups, split_rows, *refs):
    i = pl.program_id(0)
    if split_rows:
        head_ref, tail_ref, *refs = refs
        x = jnp.where(i < n_sample_blocks, head_ref[...], tail_ref[...])
    else:
        y_ref, *refs = refs
        x = y_ref[...]
    (g_ref, win_ref, bin_ref, lng_ref, lnb_ref, wsp_ref, bsp_ref, wout_ref, bout_ref,
     y2_ref, v_ref) = refs
    xn = _rms(x, g_ref[...]).astype(BF16)
    z = _dot(xn, win_ref[...]) + bin_ref[...]
    z = 0.5 * z * (1.0 + lax.erf(z * (1.0 / math.sqrt(2.0))))
    half = z.shape[1] // 2
    u, v = z[:, :half], z[:, half:]
    mu = jnp.mean(v, axis=-1, keepdims=True)
    vc = v - mu
    vn = vc * lax.rsqrt(jnp.mean(vc * vc, axis=-1, keepdims=True) + EPS) * lng_ref[...] + lnb_ref[...]

    @pl.when(i < n_sample_blocks)
    def _():
        v_ref[...] = vn

    vb = vn.astype(BF16)
    dg = half // n_groups
    rows = []
    for c in range(x.shape[0] // CHUNK):
        r0 = c * CHUNK
        cols = []
        for g in range(n_groups):
            gate = _dot(wsp_ref[0, g], vb[r0:r0 + CHUNK, g * dg:(g + 1) * dg])
            gate = gate + bsp_ref[0, :, g * dg:(g + 1) * dg]
            cols.append(u[r0:r0 + CHUNK, g * dg:(g + 1) * dg] * gate)
        rows.append(jnp.concatenate(cols, axis=1))
    out = jnp.concatenate(rows, axis=0).astype(BF16)
    y2_ref[...] = x + _dot(out, wout_ref[...]) + bout_ref[...]


def _sgu_call(y, n_sample_rows, g, w_in, b_in, ln_g, ln_b, wsp, bsp, w_out, b_out):
    rb = ROW_BLOCK
    nsb = n_sample_rows // rb
    split = isinstance(y, tuple)
    if split:
        t, d = y[0].shape[0] + y[1].shape[0], y[0].shape[1]
        row_specs = [pl.BlockSpec((rb, d), lambda i: (jnp.minimum(i, nsb - 1), 0)),
                     pl.BlockSpec((rb, d), lambda i: (jnp.maximum(i - nsb, 0), 0))]
    else:
        t, d = y.shape
        y = (y,)
        row_specs = [pl.BlockSpec((rb, d), lambda i: (i, 0))]
    n_groups = wsp.shape[1]
    half = w_out.shape[0]
    const = lambda shape: pl.BlockSpec(shape, lambda i: (0,) * len(shape))
    which = lambda i: jnp.where(i < nsb, 0, 1)
    return pl.pallas_call(
        functools.partial(_sgu_body, nsb, n_groups, split),
        grid=(t // rb,),
        in_specs=row_specs + [
            const((1, d)), const(w_in.shape), const((1, 2 * half)), const((1, half)), const((1, half)),
            pl.BlockSpec((1, n_groups, CHUNK, CHUNK), lambda i: (which(i), 0, 0, 0)),
            pl.BlockSpec((1, CHUNK, half), lambda i: (which(i), 0, 0)),
            const(w_out.shape), const((1, d)),
        ],
        out_specs=[
            pl.BlockSpec((rb, d), lambda i: (i, 0)),
            pl.BlockSpec((rb, half), lambda i: (jnp.minimum(i, nsb - 1), 0)),
        ],
        out_shape=[jax.ShapeDtypeStruct((t, d), F32), jax.ShapeDtypeStruct((n_sample_rows, half), F32)],
        compiler_params=_params("arbitrary"),
        name="sgu",
    )(*y, g, w_in, b_in, ln_g, ln_b, wsp, bsp, w_out, b_out)


def _head_mean_square(z, ones_bd, head_dim):
    sq = z * z
    tiles = z.shape[1] // LANES
    st = jnp.concatenate([sq[:, c * LANES:(c + 1) * LANES] for c in range(tiles)], axis=0)
    hi = st.astype(BF16)
    lo = (st - hi.astype(F32)).astype(BF16)
    ms = (_dot(hi, ones_bd) + _dot(lo, ones_bd)) * (1.0 / head_dim)
    r = z.shape[0]
    return jnp.concatenate([ms[c * r:(c + 1) * r] for c in range(tiles)], axis=1)


def _qkv(x, g_ref, wqkv_ref, bqkv_ref, qg_ref, kg_ref, ones_ref, n_q, n_k, head_dim):
    xn = _rms(x, g_ref[...]).astype(BF16)
    qkv = _dot(xn, wqkv_ref[...]) + bqkv_ref[...]
    q, k, v = qkv[:, :n_q], qkv[:, n_q:n_q + n_k], qkv[:, n_q + n_k:]
    ones_bd = ones_ref[...]
    q = q * lax.rsqrt(_head_mean_square(q, ones_bd, head_dim) + EPS) * qg_ref[...]
    k = k * lax.rsqrt(_head_mean_square(k, ones_bd, head_dim) + EPS) * kg_ref[...]
    return q, k, v


def _swa_prompt_body(n_kv, q_per_kv, head_dim, y_ref, g_ref, wqkv_ref, bqkv_ref, qg_ref, kg_ref,
                     ones_ref, bias_ref, sink_ref, wo_ref, bo_ref, y2_ref, ko_ref, vo_ref,
                     kprev, vprev):
    n = pl.program_id(1)
    n_q, n_k = n_kv * q_per_kv * head_dim, n_kv * head_dim
    x = y_ref[...]
    q, k, v = _qkv(x, g_ref, wqkv_ref, bqkv_ref, qg_ref, kg_ref, ones_ref, n_q, n_k, head_dim)
    qb = (q * (head_dim ** -0.5)).astype(BF16)

    @pl.when(n == 0)
    def _():
        kprev[...] = jnp.zeros_like(kprev)
        vprev[...] = jnp.zeros_like(vprev)

    w = WINDOW
    low_half = lax.broadcasted_iota(jnp.int32, (w, LANES), 1) < head_dim

    def lo_hi(z, j):
        tile = z[:, (j // 2) * LANES:(j // 2 + 1) * LANES]
        same = jnp.where(low_half == (j % 2 == 0), tile, 0.0)
        other = pltpu.roll(same, head_dim, 1)
        lo, hi = (same, other) if j % 2 == 0 else (other, same)
        return lo.astype(BF16), hi.astype(BF16)

    pairs = q_per_kv // 2
    blk = pairs * w
    col = lax.broadcasted_iota(jnp.int32, (2 * n_kv * blk, 2 * w), 1)
    no_prev = jnp.logical_and(n == 0, col < w)
    sink = sink_ref[...]
    k_prev = [(kprev[j, 0], kprev[j, 1]) for j in range(n_kv)]
    v_prev = [(vprev[j, 0], vprev[j, 1]) for j in range(n_kv)]
    att_rows = []
    for c in range(x.shape[0] // w):
        rows = slice(c * w, (c + 1) * w)
        scores, values = [], []
        for j in range(n_kv):
            k_cur, v_cur = lo_hi(k[rows], j), lo_hi(v[rows], j)
            qj = jnp.concatenate([qb[rows, (j * pairs + a) * LANES:(j * pairs + a + 1) * LANES]
                                  for a in range(pairs)], axis=0)
            for half in range(2):
                scores.append(_dot_nt(qj, jnp.concatenate([k_prev[j][half], k_cur[half]], axis=0)))
            values.append(jnp.concatenate([v_prev[j][0], v_cur[0], v_prev[j][1], v_cur[1]], axis=0))
            k_prev[j], v_prev[j] = k_cur, v_cur
        bias = bias_ref[...]
        if c == 0:
            bias = jnp.where(no_prev, NEG_INF, bias)
        s = jnp.concatenate(scores, axis=0) + bias
        m = jnp.maximum(jnp.max(s, axis=1, keepdims=True), sink)
        p = jnp.exp(s - m)
        den = jnp.sum(p, axis=1, keepdims=True) + jnp.exp(sink - m)
        p = (p * (1.0 / den)).astype(BF16)
        tiles = []
        for j in range(n_kv):
            probs = jnp.concatenate([p[(2 * j) * blk:(2 * j + 1) * blk],
                                     p[(2 * j + 1) * blk:(2 * j + 2) * blk]], axis=1)
            o = _dot(probs, values[j])
            tiles.extend(o[a * w:(a + 1) * w] for a in range(pairs))
        att_rows.append(jnp.concatenate(tiles, axis=1))
    att = jnp.concatenate(att_rows, axis=0).astype(BF16)
    y2_ref[...] = x + _dot(att, wo_ref[...]) + bo_ref[...]
    for j in range(n_kv):
        for half in range(2):
            kprev[j, half] = k_prev[j][half]
            vprev[j, half] = v_prev[j][half]
    last = x.shape[0] - w

    @pl.when(n == pl.num_programs(1) - 1)
    def _():
        ko_ref[0] = k[last:]
        vo_ref[0] = v[last:]


def _swa_prompt_call(y, n_sample_rows, batch, seq, g, w_qkv, b_qkv, qg, kg, ones_bd, bias, sink,
                     w_o, b_o, n_kv, q_per_kv, head_dim):
    t, d = y.shape
    rb = ROW_BLOCK
    nb = seq // rb
    base = n_sample_rows // rb
    n_k = n_kv * head_dim
    const = lambda shape: pl.BlockSpec(shape, lambda b, n: (0,) * len(shape))
    row_spec = pl.BlockSpec((rb, d), lambda b, n: (base + b * nb + n, 0))
    kv_spec = pl.BlockSpec((1, WINDOW, n_k), lambda b, n: (b, 0, 0))
    return pl.pallas_call(
        functools.partial(_swa_prompt_body, n_kv, q_per_kv, head_dim),
        grid=(batch, nb),
        in_specs=[row_spec, const((1, d)), const(w_qkv.shape), const(b_qkv.shape), const(qg.shape),
                  const(kg.shape), const(ones_bd.shape), const(bias.shape), const(sink.shape),
                  const(w_o.shape), const((1, d))],
        out_specs=[row_spec, kv_spec, kv_spec],
        out_shape=[jax.ShapeDtypeStruct((t, d), F32),
                   jax.ShapeDtypeStruct((batch, WINDOW, n_k), F32),
                   jax.ShapeDtypeStruct((batch, WINDOW, n_k), F32)],
        scratch_shapes=[pltpu.VMEM((n_kv, 2, WINDOW, LANES), BF16),
                        pltpu.VMEM((n_kv, 2, WINDOW, LANES), BF16)],
        input_output_aliases={0: 0},
        compiler_params=_params("arbitrary", "arbitrary"),
        name="swa_prompt",
    )(y, g, w_qkv, b_qkv, qg, kg, ones_bd, bias, sink, w_o, b_o)


def _swa_sample_body(n_kv, q_per_kv, head_dim, dec_seq, y_ref, ck_ref, cv_ref, g_ref, wqkv_ref,
                     bqkv_ref, qg_ref, kg_ref, ones_ref, biasc_ref, biasn_ref, sink_ref, wo_ref,
                     bo_ref, y2_ref, ko_ref, vo_ref, q_scr, k_scr, v_scr, att_scr):
    n_q, n_k = n_kv * q_per_kv * head_dim, n_kv * head_dim
    x = y_ref[...]
    q, k, v = _qkv(x, g_ref, wqkv_ref, bqkv_ref, qg_ref, kg_ref, ones_ref, n_q, n_k, head_dim)
    q_scr[...] = q * (head_dim ** -0.5)
    k_scr[...] = k
    v_scr[...] = v
    cw = ck_ref.shape[1]
    pad = jnp.zeros((SUBLANES - dec_seq, n_k), F32)

    per_group = SUBLANES // dec_seq
    blk = q_per_kv * dec_seq

    def per_group_of_rows(p, carry):
        r = pl.multiple_of(p * SUBLANES, SUBLANES)
        q8, k8, v8 = q_scr[pl.ds(r, SUBLANES), :], k_scr[pl.ds(r, SUBLANES), :], v_scr[pl.ds(r, SUBLANES), :]
        s_cache, s_new, values = [], [], []
        for u in range(per_group):
            s = p * per_group + u
            rows = slice(u * dec_seq, (u + 1) * dec_seq)
            ck, cv = ck_ref[s], cv_ref[s]
            k_new, v_new = k8[rows], v8[rows]
            ko_ref[s, 0:cw - dec_seq, :] = ck[dec_seq:]
            ko_ref[s, cw - dec_seq:cw, :] = k_new
            vo_ref[s, 0:cw - dec_seq, :] = cv[dec_seq:]
            vo_ref[s, cw - dec_seq:cw, :] = v_new
            qs = q8[rows].astype(BF16)
            ckb, cvb = ck.astype(BF16), cv.astype(BF16)
            kn8 = jnp.concatenate([k_new, pad], axis=0).astype(BF16)
            vn8 = jnp.concatenate([v_new, pad], axis=0).astype(BF16)
            for j in range(n_kv):
                lo, hi = j * head_dim, (j + 1) * head_dim
                qj = jnp.concatenate(
                    [qs[:, (j * q_per_kv + g) * head_dim:(j * q_per_kv + g + 1) * head_dim]
                     for g in range(q_per_kv)], axis=0)
                s_cache.append(_dot_nt(qj, ckb[:, lo:hi]))
                s_new.append(_dot_nt(qj, kn8[:, lo:hi]))
                values.append((cvb[:, lo:hi], vn8[:, lo:hi]))
        s_c = jnp.concatenate(s_cache, axis=0) + biasc_ref[...]
        s_n = jnp.concatenate(s_new, axis=0) + biasn_ref[...]
        sink = sink_ref[...]
        m = jnp.maximum(jnp.maximum(jnp.max(s_c, axis=1, keepdims=True),
                                    jnp.max(s_n, axis=1, keepdims=True)), sink)
        p_c, p_n = jnp.exp(s_c - m), jnp.exp(s_n - m)
        inv = 1.0 / (jnp.sum(p_c, axis=1, keepdims=True) + jnp.sum(p_n, axis=1, keepdims=True)
                     + jnp.exp(sink - m))
        p_c, p_n = (p_c * inv).astype(BF16), (p_n * inv).astype(BF16)
        seq_rows = []
        for u in range(per_group):
            heads = []
            for j in range(n_kv):
                b = u * n_kv + j
                o = (_dot(p_c[b * blk:(b + 1) * blk], values[b][0])
                     + _dot(p_n[b * blk:(b + 1) * blk], values[b][1]))
                heads.extend(o[g * dec_seq:(g + 1) * dec_seq] for g in range(q_per_kv))
            seq_rows.append(jnp.concatenate(heads, axis=1))
        att_scr[pl.ds(r, SUBLANES), :] = jnp.concatenate(seq_rows, axis=0)
        return carry

    lax.fori_loop(0, ck_ref.shape[0] // per_group, per_group_of_rows, 0)
    y2_ref[...] = x + _dot(att_scr[...].astype(BF16), wo_ref[...]) + bo_ref[...]


def _swa_sample_call(y, ck, cv, dec_seq, g, w_qkv, b_qkv, qg, kg, ones_bd, bias_c, bias_n, sink,
                     w_o, b_o, n_kv, q_per_kv, head_dim):
    t, d = y.shape
    dec_batch, cw, n_k = ck.shape
    sb = min(SAMPLE_SEQ_BLOCK, dec_batch)
    rb = sb * dec_seq
    n_q = n_kv * q_per_kv * head_dim
    const = lambda shape: pl.BlockSpec(shape, lambda i: (0,) * len(shape))
    row_spec = pl.BlockSpec((rb, d), lambda i: (i, 0))
    cache_spec = pl.BlockSpec((sb, cw, n_k), lambda i: (i, 0, 0))
    return pl.pallas_call(
        functools.partial(_swa_sample_body, n_kv, q_per_kv, head_dim, dec_seq),
        grid=(dec_batch // sb,),
        in_specs=[row_spec, cache_spec, cache_spec, const((1, d)), const(w_qkv.shape),
                  const(b_qkv.shape), const(qg.shape), const(kg.shape), const(ones_bd.shape),
                  const(bias_c.shape), const(bias_n.shape), const(sink.shape), const(w_o.shape),
                  const((1, d))],
        out_specs=[row_spec, cache_spec, cache_spec],
        out_shape=[jax.ShapeDtypeStruct((t, d), F32),
                   jax.ShapeDtypeStruct(ck.shape, F32), jax.ShapeDtypeStruct(cv.shape, F32)],
        scratch_shapes=[pltpu.VMEM((rb, n_q), F32), pltpu.VMEM((rb, n_k), F32),
                        pltpu.VMEM((rb, n_k), F32), pltpu.VMEM((rb, n_q), F32)],
        input_output_aliases={0: 0},
        compiler_params=_params("arbitrary"),
        name="swa_sample",
    )(y, ck, cv, g, w_qkv, b_qkv, qg, kg, ones_bd, bias_c, bias_n, sink, w_o, b_o)


def _route_body(y_ref, g_ref, wr_ref, br_ref, ids_ref, gates_ref, rank_ref, tot_ref, carry):
    i = pl.program_id(0)

    @pl.when(i == 0)
    def _():
        carry[...] = jnp.zeros_like(carry)

    xn = _rms(y_ref[...], g_ref[...])
    logits = lax.dot_general(wr_ref[...], xn, (((1,), (1,)), ((), ())),
                             precision=lax.Precision.HIGHEST, preferred_element_type=F32)
    logits = logits + br_ref[:, 0:1]
    n_exp, tb = logits.shape
    e_iota = lax.broadcasted_iota(jnp.int32, (n_exp, tb), 0)
    work = logits
    vals, ids, hots = [], [], []
    for _ in range(TOP_K):
        m = jnp.max(work, axis=0, keepdims=True)
        idx = jnp.min(jnp.where(work == m, e_iota, n_exp), axis=0, keepdims=True)
        hot = e_iota == idx
        vals.append(m)
        ids.append(idx)
        hots.append(hot)
        work = jnp.where(hot, -jnp.inf, work)
    ex = [jnp.exp(v - vals[0]) for v in vals]
    den = ex[0] + ex[1] + ex[2] + ex[3]
    ids_ref[...] = jnp.concatenate(ids, axis=0)
    gates_ref[...] = jnp.concatenate([e / den for e in ex], axis=0)

    hot_all = jnp.zeros((n_exp, tb), F32)
    for hot in hots:
        hot_all = hot_all + hot.astype(F32)
    before = (lax.broadcasted_iota(jnp.int32, (tb, tb), 0)
              < lax.broadcasted_iota(jnp.int32, (tb, tb), 1)).astype(BF16)
    count = _dot(hot_all.astype(BF16), before) + carry[:, 0:1]
    rank_ref[...] = jnp.concatenate(
        [jnp.sum(jnp.where(hot, count, 0.0), axis=0, keepdims=True) for hot in hots],
        axis=0).astype(jnp.int32)
    total = carry[...] + jnp.sum(hot_all, axis=1, keepdims=True)
    carry[...] = total
    tot_ref[...] = total.astype(jnp.int32)


def _route_call(y, g, w_r_t, b_r):
    t, d = y.shape
    n_exp = w_r_t.shape[0]
    tb = ROUTE_BLOCK
    const = lambda shape: pl.BlockSpec(shape, lambda i: (0,) * len(shape))
    tok = lambda dtype: jax.ShapeDtypeStruct((TOP_K, t), dtype)
    tok_spec = pl.BlockSpec((TOP_K, tb), lambda i: (0, i))
    return pl.pallas_call(
        _route_body,
        grid=(t // tb,),
        in_specs=[pl.BlockSpec((tb, d), lambda i: (i, 0)), const((1, d)), const((n_exp, d)),
                  const((n_exp, LANES))],
        out_specs=[tok_spec, tok_spec, tok_spec, const((n_exp, LANES))],
        out_shape=[tok(jnp.int32), tok(F32), tok(jnp.int32),
                   jax.ShapeDtypeStruct((n_exp, LANES), jnp.int32)],
        scratch_shapes=[pltpu.VMEM((n_exp, LANES), F32)],
        compiler_params=_params("arbitrary"),
        name="moe_route",
    )(y, g, w_r_t, b_r)


def _store_row_tiles(ref, x):
    n = x.shape[0]
    for s in range(SUBLANES):
        ref[pl.ds(s, n, stride=SUBLANES), :] = x[:, s * LANES:(s + 1) * LANES]


def _load_row_tiles(ref, n):
    return jnp.concatenate([ref[pl.ds(s, n, stride=SUBLANES), :] for s in range(SUBLANES)], axis=1)


def _row_tile(ref, row):
    return ref.at[pl.ds(pl.multiple_of(row * SUBLANES, SUBLANES), SUBLANES)]


def _dispatch_body(tb, tile, ztile_ref, pos_ref, y_ref, g_ref, xs_hbm, xn_scr, zeros, sem, zsem):
    i = pl.program_id(0)
    tile8 = tile * SUBLANES

    @pl.when(i == 0)
    def _():
        zeros[...] = jnp.zeros_like(zeros)

        def fill(e):
            start = pl.multiple_of(ztile_ref[e] * SUBLANES, tile8)
            return pltpu.make_async_copy(zeros, xs_hbm.at[pl.ds(start, tile8)], zsem)

        for e in range(ztile_ref.shape[0]):
            @pl.when(ztile_ref[e] >= 0)
            def _():
                fill(e).start()
        for e in range(ztile_ref.shape[0]):
            @pl.when(ztile_ref[e] >= 0)
            def _():
                fill(e).wait()

    slot = i % 2
    rows = xn_scr.at[slot]

    def drain(s):
        for k in range(TOP_K):
            pltpu.make_async_copy(xn_scr.at[s], xs_hbm.at[pl.ds(0, tb * SUBLANES)], sem.at[s]).wait()

    @pl.when(i >= 2)
    def _():
        drain(slot)

    _store_row_tiles(rows, _rms(y_ref[...], g_ref[...]))

    def issue(t, carry):
        for k in range(TOP_K):
            pltpu.make_async_copy(_row_tile(rows, t), _row_tile(xs_hbm, pos_ref[k, t]),
                                  sem.at[slot]).start(priority=k % DMA_THREADS)
        return carry

    lax.fori_loop(0, tb, issue, 0, unroll=8)

    @pl.when(i == pl.num_programs(0) - 1)
    def _():
        @pl.when(i >= 1)
        def _():
            drain(1 - slot)

        drain(slot)


def _dispatch_call(pos, ztile, y, g, n_rows_padded):
    t, d = y.shape
    tb = DISPATCH_BLOCK
    tile = EXPERT_TILE
    return pl.pallas_call(
        functools.partial(_dispatch_body, tb, tile),
        grid_spec=pltpu.PrefetchScalarGridSpec(
            num_scalar_prefetch=1,
            grid=(t // tb,),
            in_specs=[pl.BlockSpec((TOP_K, tb), lambda i, z: (0, i), memory_space=pltpu.SMEM),
                      pl.BlockSpec((tb, d), lambda i, z: (i, 0)),
                      pl.BlockSpec((1, d), lambda i, z: (0, 0))],
            out_specs=pl.BlockSpec(memory_space=pl.ANY),
            scratch_shapes=[pltpu.VMEM((2, tb * SUBLANES, LANES), F32),
                            pltpu.VMEM((tile * SUBLANES, LANES), F32),
                            pltpu.SemaphoreType.DMA((2,)), pltpu.SemaphoreType.DMA]),
        out_shape=jax.ShapeDtypeStruct((n_rows_padded * SUBLANES, LANES), F32),
        compiler_params=_params("arbitrary"),
        name="moe_dispatch",
    )(ztile, pos, y, g)


def _expert_body(tile, max_tiles, tiles_ref, start_ref, used_ref, xs_hbm, w1_ref, b1_ref, w2_ref,
                 b2_ref, sel_ref, ys_hbm, w1p, w2b, xbuf, obuf, zeros, sem_in, sem_out, zsem):
    e = pl.program_id(0)
    nt = tiles_ref[e]
    d_ff2 = w1p.shape[1]
    pair = 2 * LANES
    tile8 = tile * SUBLANES

    def rows_of(tile_index):
        start = pl.multiple_of((start_ref[e] + tile_index * tile) * SUBLANES, tile8)
        return pl.ds(start, tile8)

    def in_copy(t, slot):
        return pltpu.make_async_copy(xs_hbm.at[rows_of(t)], xbuf.at[slot, pl.ds((t % 2) * tile8, tile8)],
                                     sem_in.at[slot])

    def out_copy(t, slot):
        return pltpu.make_async_copy(obuf.at[slot, pl.ds((t % 2) * tile8, tile8)], ys_hbm.at[rows_of(t)],
                                     sem_out.at[slot])

    tile_thread = DMA_THREADS - 1

    def for_item_tiles(item, slot, action):
        action(2 * item, slot)

        @pl.when(2 * item + 1 < nt)
        def _():
            action(2 * item + 1, slot)

    def compute(slot, n_rows):
        x = _load_row_tiles(xbuf.at[slot, pl.ds(0, n_rows * SUBLANES)], n_rows).astype(BF16)
        h = _dot(x, w1p[...]) + b1_ref[0, 0]
        acts = []
        for c in range(d_ff2 // pair):
            gate = jnp.minimum(h[:, c * pair:c * pair + LANES], SWIGLU_LIMIT)
            up = jnp.clip(h[:, c * pair + LANES:(c + 1) * pair], -SWIGLU_LIMIT, SWIGLU_LIMIT)
            acts.append((up + 1.0) * (gate * (1.0 / (1.0 + jnp.exp(-SWIGLU_ALPHA * gate)))))
        a = jnp.concatenate(acts, axis=1).astype(BF16)
        _store_row_tiles(obuf.at[slot, pl.ds(0, n_rows * SUBLANES)], _dot(a, w2b[...]) + b2_ref[0, 0])

    def run_item(item, n_rows):
        slot = item % 2

        @pl.when(2 * item + 2 < nt)
        def _():
            for_item_tiles(item + 1, 1 - slot, lambda t, s: in_copy(t, s).start(priority=tile_thread))

        for_item_tiles(item, slot, lambda t, s: in_copy(t, s).wait())
        compute(slot, n_rows)
        for_item_tiles(item, slot, lambda t, s: out_copy(t, s).start(priority=tile_thread))

        @pl.when(item >= 1)
        def _():
            for_item_tiles(item - 1, 1 - slot, lambda t, s: out_copy(t, s).wait())

    @pl.when(nt > 0)
    def _():
        for_item_tiles(0, 0, lambda t, s: in_copy(t, s).start(priority=tile_thread))
        sel = sel_ref[...]
        for c in range(d_ff2 // pair):
            blk = w1_ref[0, 0, :, c * pair:(c + 1) * pair].astype(BF16)
            w1p[:, c * pair:(c + 1) * pair] = _dot(blk, sel).astype(BF16)
        w2b[...] = w2_ref[0, 0].astype(BF16)

        n_full = nt // 2

        def step(item, carry):
            run_item(item, 2 * tile)
            return carry

        lax.fori_loop(0, n_full, step, 0)

        @pl.when(nt % 2 == 1)
        def _():
            run_item(n_full, tile)

        last = (nt - 1) // 2
        for_item_tiles(last, last % 2, lambda t, s: out_copy(t, s).wait())

    @pl.when(e == pl.num_programs(0) - 1)
    def _():
        zeros[...] = jnp.zeros_like(zeros)
        spare = max_tiles - used_ref[0]

        def fill(s):
            start = pl.multiple_of((used_ref[0] + s) * tile8, tile8)
            return pltpu.make_async_copy(zeros, ys_hbm.at[pl.ds(start, tile8)], zsem)

        def start_fill(s, carry):
            fill(s).start()
            return carry

        def wait_fill(s, carry):
            fill(s).wait()
            return carry

        lax.fori_loop(0, spare, start_fill, 0)
        lax.fori_loop(0, spare, wait_fill, 0)


def _expert_call(layer, tiles_e, row_start, n_tiles, xs, w1, b1p, w2, b2, sel):
    tile = EXPERT_TILE
    tile8 = tile * SUBLANES
    max_tiles = xs.shape[0] // tile8
    n_exp, d, d_ff2 = w1.shape[1], w1.shape[2], w1.shape[3]
    d_ff = w2.shape[2]
    exp4 = lambda e, *_: (layer, e, 0, 0)
    any_spec = pl.BlockSpec(memory_space=pl.ANY)
    return pl.pallas_call(
        functools.partial(_expert_body, tile, max_tiles),
        grid_spec=pltpu.PrefetchScalarGridSpec(
            num_scalar_prefetch=3,
            grid=(n_exp,),
            in_specs=[any_spec,
                      pl.BlockSpec((1, 1, d, d_ff2), exp4), pl.BlockSpec((1, 1, 1, d_ff2), exp4),
                      pl.BlockSpec((1, 1, d_ff, d), exp4), pl.BlockSpec((1, 1, 1, d), exp4),
                      pl.BlockSpec(sel.shape, lambda e, *_: (0, 0))],
            out_specs=any_spec,
            scratch_shapes=[pltpu.VMEM((d, d_ff2), BF16), pltpu.VMEM((d_ff, d), BF16),
                            pltpu.VMEM((2, 2 * tile8, LANES), F32),
                            pltpu.VMEM((2, 2 * tile8, LANES), F32),
                            pltpu.VMEM((tile8, LANES), F32),
                            pltpu.SemaphoreType.DMA((2,)), pltpu.SemaphoreType.DMA((2,)),
                            pltpu.SemaphoreType.DMA]),
        out_shape=jax.ShapeDtypeStruct(xs.shape, F32),
        compiler_params=_params("arbitrary"),
        name="moe_expert",
    )(tiles_e, row_start, n_tiles, xs, w1, b1p, w2, b2, sel)


def _combine_body(tb, n_head_blocks, pos_ref, posn_ref, gates_ref, y_ref, ys_hbm, *refs):
    *out_refs, buf, sem = refs
    i = pl.program_id(0)
    nsteps = pl.num_programs(0)
    slot = i % 2

    def start_row(table, dst_slot, t):
        for k in range(TOP_K):
            pltpu.make_async_copy(_row_tile(ys_hbm, table[k, t]), _row_tile(buf.at[dst_slot, k], t),
                                  sem.at[dst_slot]).start(priority=k % DMA_THREADS)

    def wait_block(s):
        for k in range(TOP_K):
            pltpu.make_async_copy(ys_hbm.at[pl.ds(0, tb * SUBLANES)], buf.at[s, k], sem.at[s]).wait()

    @pl.when(i == 0)
    def _():
        def one(t, carry):
            start_row(pos_ref, 0, t)
            return carry
        lax.fori_loop(0, tb, one, 0, unroll=8)

    for t in range(tb):
        start_row(posn_ref, 1 - slot, t)

    wait_block(slot)
    gate = [jnp.broadcast_to(gates_ref[:, k:k + 1], (tb, LANES)) for k in range(TOP_K)]
    accs = []
    for s in range(SUBLANES):
        acc = y_ref[:, s * LANES:(s + 1) * LANES]
        for k in range(TOP_K):
            acc = acc + gate[k] * buf[slot, k, pl.ds(s, tb, stride=SUBLANES), :]
        accs.append(acc)

    def write(ref):
        for s in range(SUBLANES):
            ref[:, s * LANES:(s + 1) * LANES] = accs[s]

    if n_head_blocks is None:
        write(out_refs[0])
    else:
        pl.when(i < n_head_blocks)(lambda: write(out_refs[0]))
        pl.when(i >= n_head_blocks)(lambda: write(out_refs[1]))

    @pl.when(i == nsteps - 1)
    def _():
        wait_block(1 - slot)


def _combine_call(pos, gates_t, y, ys, n_head_rows=None):
    t, d = y.shape
    tb = COMBINE_BLOCK
    nb = t // tb
    smem = lambda fn: pl.BlockSpec((TOP_K, tb), fn, memory_space=pltpu.SMEM)
    if n_head_rows is None:
        nhb = None
        out_specs = pl.BlockSpec((tb, d), lambda i: (i, 0))
        out_shape = jax.ShapeDtypeStruct((t, d), F32)
    else:
        nhb = n_head_rows // tb
        out_specs = [pl.BlockSpec((tb, d), lambda i: (jnp.minimum(i, nhb - 1), 0)),
                     pl.BlockSpec((tb, d), lambda i: (jnp.maximum(i - nhb, 0), 0))]
        out_shape = [jax.ShapeDtypeStruct((n_head_rows, d), F32),
                     jax.ShapeDtypeStruct((t - n_head_rows, d), F32)]
    return pl.pallas_call(
        functools.partial(_combine_body, tb, nhb),
        grid=(nb,),
        in_specs=[smem(lambda i: (0, i)), smem(lambda i: (0, jnp.minimum(i + 1, nb - 1))),
                  pl.BlockSpec((tb, TOP_K), lambda i: (i, 0)),
                  pl.BlockSpec((tb, d), lambda i: (i, 0)),
                  pl.BlockSpec(memory_space=pl.ANY)],
        out_specs=out_specs,
        out_shape=out_shape,
        scratch_shapes=[pltpu.VMEM((2, TOP_K, tb * SUBLANES, LANES), F32),
                        pltpu.SemaphoreType.DMA((2,))],
        compiler_params=_params("arbitrary"),
        name="moe_combine",
    )(pos, pos, gates_t, y, ys)


def _moe(layer, y, g, w_r, b_r, w1, b1p, w2, b2, sel, n_head_rows=None):
    t, d = y.shape
    n_exp = w_r.shape[1]
    tile = EXPERT_TILE
    max_tiles = (TOP_K * t) // tile + n_exp
    b_r_l = jnp.broadcast_to(b_r[:, None], (n_exp, LANES))
    ids, gates, rank, totals = _route_call(y, g, w_r.T, b_r_l)

    n_e = totals[:, 0]
    tiles_e = (n_e + tile - 1) // tile
    tile_end = jnp.cumsum(tiles_e)
    row_start = (tile_end - tiles_e) * tile
    n_tiles = tile_end[-1:]
    pos = rank
    for e in range(n_exp):
        pos = pos + jnp.where(ids == e, row_start[e], 0)
    spare = n_tiles[0] + jnp.arange(n_exp, dtype=jnp.int32)
    ztile = jnp.concatenate([jnp.where(n_e % tile != 0, (tile_end - 1) * tile, -1),
                             jnp.where(spare < max_tiles, spare * tile, -1)]).astype(jnp.int32)

    xs = _dispatch_call(pos, ztile, y, g, max_tiles * tile)
    ys = _expert_call(layer, tiles_e.astype(jnp.int32), row_start.astype(jnp.int32),
                      n_tiles.astype(jnp.int32), xs, w1, b1p, w2, b2, sel)
    return _combine_call(pos, gates.T, y, ys, n_head_rows)


def _t5_bucket(dist):
    n = np.maximum(dist, 0)
    max_exact = NUM_BUCKETS // 2
    nf = np.maximum(n, max_exact).astype(np.float32)
    large = max_exact + (np.log(nf / np.float32(max_exact))
                         / np.float32(math.log(MAX_DISTANCE / max_exact))
                         * np.float32(NUM_BUCKETS - max_exact)).astype(np.int32)
    large = np.minimum(large, NUM_BUCKETS - 1)
    return np.where(n < max_exact, n, large)


def _window_bias(dist, rel_bias, n_kv, q_per_kv):
    tq, tk = dist.shape
    hot = (_t5_bucket(dist)[..., None] == np.arange(NUM_BUCKETS)).astype(np.float32)
    bias = jnp.einsum("qkb,bh->hqk", hot, rel_bias.astype(F32), precision=lax.Precision.HIGHEST)
    valid = (dist >= 0) & (dist < WINDOW)
    return jnp.where(valid, bias, NEG_INF).reshape(n_kv, q_per_kv * tq, tk)


def _selection_matrix():
    src = jnp.arange(2 * LANES)
    dst = (src % 2) * LANES + src // 2
    return jax.nn.one_hot(dst, 2 * LANES, dtype=BF16)


def kernel(x_prompt, x_sample, cache_k, cache_v, norm_mix_g, norm_ffn_g, a_w_in, a_b_in, a_ln_g,
           a_ln_b, a_w_s, a_b_s, a_w_out, a_b_out, s_w_qkv, s_b_qkv, s_q_norm_g, s_k_norm_g,
           s_sinks, s_w_o, s_b_o, rel_bias, m_w_router, m_b_router, m_w1, m_b1, m_w2, m_b2):
    batch, seq, d = x_prompt.shape
    dec_batch, dec_seq, _ = x_sample.shape
    depth = norm_mix_g.shape[0]
    n_groups = a_w_s.shape[1]
    n_heads = s_sinks.shape[1]
    n_kv, head_dim = cache_k.shape[3], cache_k.shape[4]
    q_per_kv = n_heads // n_kv
    cw = cache_k.shape[2]
    ns = dec_batch * dec_seq
    assert d == SUBLANES * LANES, "the routed rows are moved as one (8, 128) tile per token"
    assert 2 * head_dim == LANES and q_per_kv % 2 == 0, "two heads share one lane tile"
    assert ns % ROW_BLOCK == 0 and seq % ROW_BLOCK == 0 and CHUNK % dec_seq == 0
    assert ns % COMBINE_BLOCK == 0 and depth >= 1
    assert (ns + batch * seq) % ROUTE_BLOCK == 0 and seq >= WINDOW and SUBLANES % dec_seq == 0

    y = (x_sample.reshape(ns, d), x_prompt.reshape(batch * seq, d))
    sel = _selection_matrix()
    ones_bd = jnp.kron(jnp.eye(LANES // head_dim, dtype=F32),
                       jnp.ones((head_dim, head_dim), F32)).astype(BF16)

    qi, kj = np.arange(WINDOW), np.arange(2 * WINDOW)
    bias_p = _window_bias(WINDOW + qi[:, None] - kj[None, :], rel_bias, n_kv, q_per_kv)
    by_parity = lambda a: (a.reshape(n_kv, q_per_kv // 2, 2, WINDOW, -1).swapaxes(1, 2)
                           .reshape(n_kv * q_per_kv * WINDOW, -1))
    bias_p = by_parity(bias_p)
    qt = np.arange(dec_seq)
    bias_c = _window_bias(qt[:, None] + cw - np.arange(cw)[None, :], rel_bias, n_kv, q_per_kv)
    tn = np.arange(SUBLANES)
    dist_n = np.where(tn[None, :] < dec_seq, qt[:, None] - tn[None, :], -1)
    bias_n = _window_bias(dist_n, rel_bias, n_kv, q_per_kv)
    per_group = lambda a: jnp.tile(a.reshape(n_kv * q_per_kv * dec_seq, -1), (SUBLANES // dec_seq, 1))
    bias_c, bias_n = per_group(bias_c), per_group(bias_n)

    n_exp, d_ff2 = m_b1.shape[1], m_b1.shape[2]
    b1p = (m_b1.reshape(depth, n_exp, d_ff2 // (2 * LANES), LANES, 2).swapaxes(3, 4)
           .reshape(depth, n_exp, 1, d_ff2))
    b2 = m_b2[:, :, None, :]

    row = lambda a: a.reshape(1, -1)
    new_kp, new_vp, new_ks, new_vs, new_va = [], [], [], [], []
    for i in range(depth):
        j = i // 2
        g_mix = row(norm_mix_g[i])
        if i % 2 == 0:
            w_s = jnp.tril(a_w_s[j])
            c = min(dec_seq, CHUNK)
            w_sample = jnp.einsum("ab,gpq->gapbq", jnp.eye(CHUNK // c, dtype=F32),
                                  jnp.tril(a_w_s[j][:, :c, :c])).reshape(n_groups, CHUNK, CHUNK)
            wsp = jnp.stack([w_sample, w_s[:, :CHUNK, :CHUNK]]).astype(BF16)
            dg = a_w_out.shape[1] // n_groups
            b_prompt = jnp.repeat(a_b_s[j][:, :CHUNK].T, dg, axis=1)
            b_sample = jnp.repeat(jnp.tile(a_b_s[j][:, :c].T, (CHUNK // c, 1)), dg, axis=1)
            bsp = jnp.stack([b_sample, b_prompt])
            y, v_s = _sgu_call(y, ns, g_mix, a_w_in[j].astype(BF16), row(a_b_in[j]), row(a_ln_g[j]),
                               row(a_ln_b[j]), wsp, bsp, a_w_out[j].astype(BF16), row(a_b_out[j]))
            new_va.append(v_s.reshape(dec_batch, dec_seq, -1))
        else:
            w_qkv, b_qkv = s_w_qkv[j].astype(BF16), row(s_b_qkv[j])
            qg = row(jnp.tile(s_q_norm_g[j], n_heads))
            kg = row(jnp.tile(s_k_norm_g[j], n_kv))
            w_o, b_o = s_w_o[j].astype(BF16), row(s_b_o[j])
            sinks = s_sinks[j].astype(F32).reshape(n_kv, q_per_kv, 1)
            sink_p = by_parity(jnp.repeat(sinks, WINDOW, axis=1))
            sink_s = per_group(jnp.repeat(sinks, dec_seq, axis=1))
            ck = cache_k[j].reshape(dec_batch, cw, n_kv * head_dim)
            cv = cache_v[j].reshape(dec_batch, cw, n_kv * head_dim)
            y, ks, vs = _swa_sample_call(y, ck, cv, dec_seq, g_mix, w_qkv, b_qkv, qg, kg, ones_bd,
                                         bias_c, bias_n, sink_s, w_o, b_o, n_kv, q_per_kv, head_dim)
            y, kp, vp = _swa_prompt_call(y, ns, batch, seq, g_mix, w_qkv, b_qkv, qg, kg, ones_bd,
                                         bias_p, sink_p, w_o, b_o, n_kv, q_per_kv, head_dim)
            new_ks.append(ks.reshape(dec_batch, cw, n_kv, head_dim))
            new_vs.append(vs.reshape(dec_batch, cw, n_kv, head_dim))
            new_kp.append(kp.reshape(batch, WINDOW, n_kv, head_dim))
            new_vp.append(vp.reshape(batch, WINDOW, n_kv, head_dim))
        y = _moe(i, y, row(norm_ffn_g[i]), m_w_router[i], m_b_router[i], m_w1, b1p, m_w2, b2, sel,
                 n_head_rows=ns if i == depth - 1 else None)
    ys, yp = y
    yp = yp.reshape(batch, seq, d)
    ys = ys.reshape(dec_batch, dec_seq, d)
    return (yp, ys, jnp.stack(new_kp), jnp.stack(new_vp), jnp.stack(new_ks), jnp.stack(new_vs),
            jnp.stack(new_va))
```
